```python
import math
import jax, jax.numpy as jnp
from jax import lax
import numpy as np

D_MODEL = 1024
BATCH = 32
SEQ = 256
DEPTH = 4
DEC_BATCH = 4
DEC_SEQ = 2048
PAST_LEN = 512

GRID_W = 64
N_EVEN = (DEPTH + 1) // 2
N_ODD = DEPTH // 2
D_A = D_MODEL // 2
D_B = D_MODEL - D_A
A_HEADS = 4
A_DK = D_A // A_HEADS
A_DV = D_A // A_HEADS
B_HEADS = 8
B_DH = D_B // B_HEADS
NA_ROWS = 8
NA_COLS = 16
CHUNK = 32
Q_BLOCK = 128
CONV_W = 3
D_FF = ((8 * D_MODEL // 3 + 127) // 128) * 128
N_MOD = 9
AB_IN = 5 * D_A + 3 * D_B
EPS = 1e-6
LB_MAX = 1.0 - 1e-4

kernel_name = "hybrid_hgrn2_natten_shortconv_diffusion_step"


def rmsnorm(x, g):
    xf = x.astype(jnp.float32)
    y = xf * lax.rsqrt(jnp.mean(xf * xf, axis=-1, keepdims=True) + EPS)
    return (y * g.astype(jnp.float32)).astype(x.dtype)


def adaln(cond, w, b):
    return (jax.nn.silu(cond) @ w + b)[:, None, :]


def mod_chunk(mod, i):
    return mod[..., i * D_MODEL:(i + 1) * D_MODEL]


def pre_norm(x, mod, j, g):
    return rmsnorm(x, g) * (1 + mod_chunk(mod, 3 * j + 1)) + mod_chunk(mod, 3 * j)


def post_residual(x, out, mod, j, g, w):
    return x + w * mod_chunk(mod, 3 * j + 2) * rmsnorm(out, g)


def swiglu(h, w_in, w_out):
    gate, up = jnp.split(h @ w_in, 2, axis=-1)
    return (jax.nn.silu(gate) * up) @ w_out


def hgrn2_lower_bounds(a_lb):
    p = jax.nn.softmax(a_lb.astype(jnp.float32), axis=0)
    return jnp.clip(jnp.cumsum(p, axis=0) - p[0], 0.0, LB_MAX)


def hgrn2_forget(z, lb):
    f = lb + (1.0 - lb) * jax.nn.sigmoid(z)
    log_f = jnp.maximum(jnp.log1p(-lb) + jax.nn.log_sigmoid(z), jnp.log(jnp.maximum(f, 1e-30)))
    k = (1.0 - lb) * jax.nn.sigmoid(-z)
    return log_f, k


def chunk_scan(q, k, v, log_f, s0):
    Bn, L, H, _ = q.shape
    n = L // CHUNK

    def chunks(t):
        return t.reshape(Bn, n, CHUNK, H, t.shape[-1]).transpose(1, 0, 3, 2, 4)

    causal = jnp.tril(jnp.ones((CHUNK, CHUNK), dtype=bool))[:, :, None]

    def step(S, blk):
        qc, kc, vc, gc = blk
        b = jnp.cumsum(gc, axis=2)
        diff = jnp.where(causal, b[:, :, :, None, :] - b[:, :, None, :, :], -jnp.inf)
        decay = jnp.exp(diff)
        att = jnp.einsum('bhtk,bhsk,bhtsk->bhts', qc, kc, decay)
        o = jnp.einsum('bhts,bhsv->bhtv', att, vc) + jnp.einsum('bhtk,bhkv->bhtv', qc * jnp.exp(b), S)
        b_end = b[:, :, -1:, :]
        S = (jnp.exp(b_end[:, :, 0, :])[..., None] * S
             + jnp.einsum('bhsk,bhsv->bhkv', kc * jnp.exp(b_end - b), vc))
        return S, o

    S, o = lax.scan(step, s0, (chunks(q), chunks(k), chunks(v), chunks(log_f)))
    return o.transpose(1, 0, 3, 2, 4).reshape(Bn, L, H, -1), S


def hgrn2_mixer(a_q, a_ffw, a_fbw, a_i, a_g, lb, gnorm, s0):
    Bn, L, _ = a_q.shape

    def heads(t):
        return t.astype(jnp.float32).reshape(Bn, L, A_HEADS, -1)

    q, v = heads(a_q), heads(a_i)
    logf_f, k_f = hgrn2_forget(heads(a_ffw), lb[0].reshape(A_HEADS, A_DK))
    logf_b, k_b = hgrn2_forget(heads(a_fbw), lb[1].reshape(A_HEADS, A_DK))
    o_f, s_f = chunk_scan(q, k_f, v, logf_f, s0[:, 0])

    def flip(t):
        return jnp.flip(t, axis=1)

    o_b, s_b = chunk_scan(flip(q), flip(k_b), flip(v), flip(logf_b), s0[:, 1])
    o = rmsnorm(o_f + flip(o_b), gnorm).reshape(Bn, L, D_A)
    o = o * jax.nn.silu(a_g.astype(jnp.float32))
    return o.astype(a_q.dtype), jnp.stack([s_f, s_b], axis=1)


def na_heads(t):
    Bn, L, _ = t.shape
    return t.reshape(Bn, L, B_HEADS, B_DH).transpose(0, 2, 1, 3)


def context_attention(q, k, v):
    Bn, H, L, dh = q.shape
    nb = L // Q_BLOCK
    scale = 1.0 / math.sqrt(dh)
    qb = q.reshape(Bn, H, nb, Q_BLOCK, dh).transpose(2, 0, 1, 3, 4)

    def blk(qi):
        s = jnp.einsum('bhqd,bhkd->bhqk', qi, k).astype(jnp.float32) * scale
        p = jax.nn.softmax(s, axis=-1).astype(v.dtype)
        return jnp.einsum('bhqk,bhkd->bhqd', p, v)

    o = lax.map(blk, qb)
    return o.transpose(1, 2, 0, 3, 4).reshape(Bn, H, L, dh)


def neighbourhood_attention(q, k, v, ck, cv, rpb):
    Bn, H, L, dh = q.shape
    rows = L // GRID_W
    wr = min(NA_ROWS, rows)
    scale = 1.0 / math.sqrt(dh)
    r = jnp.arange(rows)
    key_rows = jnp.clip(r - wr // 2, 0, rows - wr)[:, None] + jnp.arange(wr)[None, :]
    col = jnp.arange(GRID_W)
    c0 = jnp.clip(col - NA_COLS // 2, 0, GRID_W - NA_COLS)
    col_ok = (col[None, :] >= c0[:, None]) & (col[None, :] < c0[:, None] + NA_COLS)
    qg = q.reshape(Bn, H, rows, GRID_W, dh)
    kg = k.reshape(Bn, H, rows, GRID_W, dh)[:, :, key_rows]
    vg = v.reshape(Bn, H, rows, GRID_W, dh)[:, :, key_rows]
    s_loc = jnp.einsum('bhrqd,bhrjkd->bhrqjk', qg, kg).astype(jnp.float32) * scale
    r_idx = (key_rows - r[:, None] + NA_ROWS - 1)[:, None, :, None]
    c_idx = jnp.clip(col[None, :] - col[:, None], -(NA_COLS - 1), NA_COLS - 1) + NA_COLS - 1
    bias = rpb.astype(jnp.float32)[:, r_idx, c_idx[None, :, None, :]]
    s_loc = jnp.where(col_ok[:, None, :], s_loc + bias[None], -jnp.inf)
    s_loc = s_loc.reshape(Bn, H, rows, GRID_W, wr * GRID_W)
    s_ctx = jnp.einsum('bhrqd,bhcd->bhrqc', qg, ck).astype(jnp.float32) * scale
    p = jax.nn.softmax(jnp.concatenate([s_loc, s_ctx], axis=-1), axis=-1)
    p_loc = p[..., :wr * GRID_W].reshape(Bn, H, rows, GRID_W, wr, GRID_W).astype(v.dtype)
    p_ctx = p[..., wr * GRID_W:].astype(v.dtype)
    o = (jnp.einsum('bhrqjk,bhrjkd->bhrqd', p_loc, vg)
         + jnp.einsum('bhrqc,bhcd->bhrqd', p_ctx, cv))
    return o.reshape(Bn, H, L, dh)


def split_even(p):
    idx = [D_A, 2 * D_A, 3 * D_A, 4 * D_A, 5 * D_A, 5 * D_A + D_B, 5 * D_A + 2 * D_B]
    return jnp.split(p, idx, axis=-1)


def even_mixer_context(h, w_in, w_out, lb, gnorm):
    Bn, L, _ = h.shape
    a_q, a_ff, a_fb, a_i, a_g, b_q, b_k, b_v = split_even(h @ w_in)
    s0 = jnp.zeros((Bn, 2, A_HEADS, A_DK, A_DV), jnp.float32)
    o_a, s_fin = hgrn2_mixer(a_q, a_ff, a_fb, a_i, a_g, lb, gnorm, s0)
    q, k, v = na_heads(b_q), na_heads(b_k), na_heads(b_v)
    o_b = context_attention(q, k, v).transpose(0, 2, 1, 3).reshape(Bn, L, D_B)
    out = jnp.concatenate([o_a, o_b], axis=-1) @ w_out
    return out, s_fin, k, v


def even_mixer_latent(h, w_in, w_out, lb, gnorm, rpb, s_ctx, k_ctx, v_ctx):
    Bn, L, _ = h.shape
    a_q, a_ff, a_fb, a_i, a_g, b_q, b_k, b_v = split_even(h @ w_in)
    o_a, _ = hgrn2_mixer(a_q, a_ff, a_fb, a_i, a_g, lb, gnorm, s_ctx.astype(jnp.float32))
    o_b = neighbourhood_attention(na_heads(b_q), na_heads(b_k), na_heads(b_v), k_ctx, v_ctx, rpb)
    o_b = o_b.transpose(0, 2, 1, 3).reshape(Bn, L, D_B)
    return jnp.concatenate([o_a, o_b], axis=-1) @ w_out


def conv_mixer(h, w_in, conv_w, w_out):
    L = h.shape[1]
    b_gate, c_gate, xin = jnp.split(h @ w_in, 3, axis=-1)
    u = jnp.pad(c_gate * xin, ((0, 0), (CONV_W // 2, CONV_W // 2), (0, 0)))
    y = sum(conv_w[j] * u[:, j:j + L] for j in range(CONV_W))
    return (b_gate * y) @ w_out


def setup_inputs(seed: int = 0) -> dict:
    key = jax.random.key(seed)
    ks = jax.random.split(key, 24)
    D = D_MODEL

    def nrm(k, shape, s):
        return jax.random.normal(k, shape, jnp.float32) * s

    return {
        'x_prompt': nrm(ks[0], (BATCH, SEQ, D), 1.0),
        'x_sample': nrm(ks[1], (DEC_BATCH, DEC_SEQ, D), 1.0),
        'c': nrm(ks[2], (DEC_BATCH, D), 1.0),
        'state_hgrn': nrm(ks[3], (DEC_BATCH, N_EVEN, 2, A_HEADS, A_DK, A_DV), 0.5),
        'cache_k': nrm(ks[4], (DEC_BATCH, N_EVEN, B_HEADS, PAST_LEN, B_DH), 1.0),
        'cache_v': nrm(ks[5], (DEC_BATCH, N_EVEN, B_HEADS, PAST_LEN, B_DH), 1.0),
        'c_ctx': nrm(ks[6], (D,), 1.0),
        'mod_w': nrm(ks[7], (DEPTH, D, N_MOD * D), D ** -0.5),
        'mod_b': nrm(ks[8], (DEPTH, N_MOD * D), 0.02),
        'norm_pre': 1.0 + nrm(ks[9], (DEPTH, 3, D), 0.02),
        'norm_post': 1.0 + nrm(ks[10], (DEPTH, 3, D), 0.02),
        'ffn_w_in': nrm(ks[11], (DEPTH, 2, D, 2 * D_FF), D ** -0.5),
        'ffn_w_out': nrm(ks[12], (DEPTH, 2, D_FF, D), D_FF ** -0.5),
        'ab_w_in': nrm(ks[13], (N_EVEN, D, AB_IN), D ** -0.5),
        'ab_w_out': nrm(ks[14], (N_EVEN, D_A + D_B, D), (D_A + D_B) ** -0.5),
        'a_lb': nrm(ks[15], (N_EVEN, 2, D_A), 1.0),
        'a_gnorm': 1.0 + nrm(ks[16], (N_EVEN, A_DV), 0.02),
        'b_rpb': nrm(ks[17], (N_EVEN, B_HEADS, 2 * NA_ROWS - 1, 2 * NA_COLS - 1), 0.1),
        'cv_w_in': nrm(ks[18], (N_ODD, D, 3 * D), D ** -0.5),
        'cv_conv': nrm(ks[19], (N_ODD, CONV_W, D), CONV_W ** -0.5),
        'cv_w_out': nrm(ks[20], (N_ODD, D, D), D ** -0.5),
    }


def reference(x_prompt, x_sample, c, state_hgrn, cache_k, cache_v, c_ctx, mod_w, mod_b,
              norm_pre, norm_post, ffn_w_in, ffn_w_out, ab_w_in, ab_w_out, a_lb, a_gnorm,
              b_rpb, cv_w_in, cv_conv, cv_w_out):
    lb_all = hgrn2_lower_bounds(a_lb)
    xp, xs = x_prompt, x_sample
    new_s, new_k, new_v = [], [], []
    for l in range(DEPTH):
        mod_p = adaln(c_ctx[None, :], mod_w[l], mod_b[l])
        mod_s = adaln(c, mod_w[l], mod_b[l])
        xp = post_residual(xp, swiglu(pre_norm(xp, mod_p, 0, norm_pre[l, 0]), ffn_w_in[l, 0], ffn_w_out[l, 0]),
                           mod_p, 0, norm_post[l, 0], 0.5)
        xs = post_residual(xs, swiglu(pre_norm(xs, mod_s, 0, norm_pre[l, 0]), ffn_w_in[l, 0], ffn_w_out[l, 0]),
                           mod_s, 0, norm_post[l, 0], 0.5)
        hp = pre_norm(xp, mod_p, 1, norm_pre[l, 1])
        hs = pre_norm(xs, mod_s, 1, norm_pre[l, 1])
        j = l // 2
        if l % 2 == 0:
            op, s_fin, k_ctx, v_ctx = even_mixer_context(hp, ab_w_in[j], ab_w_out[j], lb_all[j], a_gnorm[j])
            new_s.append(s_fin)
            new_k.append(k_ctx)
            new_v.append(v_ctx)
            os_ = even_mixer_latent(hs, ab_w_in[j], ab_w_out[j], lb_all[j], a_gnorm[j], b_rpb[j],
                                    state_hgrn[:, j], cache_k[:, j], cache_v[:, j])
        else:
            op = conv_mixer(hp, cv_w_in[j], cv_conv[j], cv_w_out[j])
            os_ = conv_mixer(hs, cv_w_in[j], cv_conv[j], cv_w_out[j])
        xp = post_residual(xp, op, mod_p, 1, norm_post[l, 1], 1.0)
        xs = post_residual(xs, os_, mod_s, 1, norm_post[l, 1], 1.0)
        xp = post_residual(xp, swiglu(pre_norm(xp, mod_p, 2, norm_pre[l, 2]), ffn_w_in[l, 1], ffn_w_out[l, 1]),
                           mod_p, 2, norm_post[l, 2], 0.5)
        xs = post_residual(xs, swiglu(pre_norm(xs, mod_s, 2, norm_pre[l, 2]), ffn_w_in[l, 1], ffn_w_out[l, 1]),
                           mod_s, 2, norm_post[l, 2], 0.5)
    state_hgrn_new = jnp.stack(new_s, axis=1)
    cache_k_new = jnp.stack(new_k, axis=1)
    cache_v_new = jnp.stack(new_v, axis=1)
    return (xp, xs, state_hgrn_new, cache_k_new, cache_v_new)
```

```python
import functools
import math

import jax
import jax.numpy as jnp
from jax import lax
from jax.experimental import pallas as pl
from jax.experimental.pallas import tpu as pltpu

F32 = jnp.float32
BF16 = jnp.bfloat16

EPS = 1e-6
LB_MAX = 1.0 - 1e-4
MASKED = -1e30

GRID_W = 64
A_HEADS = 4
B_HEADS = 8
NA_ROWS = 8
NA_COLS = 16
CONV_W = 3
N_MOD = 9

VMEM_LIMIT_BYTES = 56 * 1024 * 1024
LANES = 128

ROW_TILE = 512
HALO = 16
SCAN_CHUNK = 128
SCAN_GROUP = 16
SAFE_EXP = 80.0
Q_ROWS = 8
KEY_ROWS = 16


def _params(*sem):
    return pltpu.CompilerParams(dimension_semantics=sem, vmem_limit_bytes=VMEM_LIMIT_BYTES)


def _rms(x, g):
    return x * lax.rsqrt(jnp.mean(x * x, axis=-1, keepdims=True) + EPS) * g


def _silu(x):
    return x * jax.nn.sigmoid(x)


def _dot(a, b):
    return jnp.dot(a, b, preferred_element_type=F32)


def _dot_nt(a, b):
    return lax.dot_general(a, b, (((1,), (1,)), ((), ())), preferred_element_type=F32)


def _dot_tn(a, b):
    return lax.dot_general(a, b, (((0,), (0,)), ((), ())), preferred_element_type=F32)


def _mod_parts(mod, j, d):
    return (mod[:, (3 * j) * d:(3 * j + 1) * d],
            mod[:, (3 * j + 1) * d:(3 * j + 2) * d],
            mod[:, (3 * j + 2) * d:(3 * j + 3) * d])


class _Rows:
    def __init__(self, n_ctx_rows, ctx_len, n_lat_rows, lat_len, tile):
        assert ctx_len % tile == 0 or tile % ctx_len == 0
        assert lat_len % tile == 0 and n_ctx_rows % tile == 0 and n_lat_rows % tile == 0
        self.tile = tile
        self.n_ctx_tiles = n_ctx_rows // tile
        self.n_tiles = (n_ctx_rows + n_lat_rows) // tile
        self.lat_tiles_per_batch = lat_len // tile
        self.ctx_len = ctx_len
        self.lat_len = lat_len

    def mod_row(self, i):
        return jnp.where(i < self.n_ctx_tiles, 0, 1 + (i - self.n_ctx_tiles) // self.lat_tiles_per_batch)

    def mod_spec(self, width):
        return pl.BlockSpec((None, 1, width), lambda i: (self.mod_row(i), 0, 0))


def _mod_kernel(cond_ref, w_ref, b_ref, o_ref):
    s = _silu(cond_ref[...]).astype(BF16)
    o_ref[...] = _dot(s, w_ref[...].astype(BF16)) + b_ref[...]


def _modulation(cond, mod_w, mod_b):
    depth, d, n = mod_w.shape
    rows = cond.shape[0]
    tn = 1536
    assert n % tn == 0
    return pl.pallas_call(
        _mod_kernel,
        out_shape=jax.ShapeDtypeStruct((depth, rows, n), F32),
        grid=(depth, n // tn),
        in_specs=[pl.BlockSpec((rows, d), lambda l, j: (0, 0)),
                  pl.BlockSpec((None, d, tn), lambda l, j: (l, 0, j)),
                  pl.BlockSpec((None, 1, tn), lambda l, j: (l, 0, j))],
        out_specs=pl.BlockSpec((None, rows, tn), lambda l, j: (l, 0, j)),
        compiler_params=_params("parallel", "parallel"),
        name="adaln_mod",
    )(cond, mod_w, mod_b.reshape(depth, 1, n))


def _ffn_kernel(x_ref, mod_ref, gpre_ref, gpost_ref, win_ref, wout_ref, o_ref, h_ref, act_ref, *, j, tf):
    d = x_ref.shape[1]
    dff = wout_ref.shape[0]
    shift, scale, gate = _mod_parts(mod_ref[...], j, d)
    h_ref[...] = (_rms(x_ref[...], gpre_ref[...]) * (1.0 + scale) + shift).astype(BF16)
    for c in range(dff // tf):
        g = _dot(h_ref[...], win_ref[:, c * tf:(c + 1) * tf])
        u = _dot(h_ref[...], win_ref[:, dff + c * tf:dff + (c + 1) * tf])
        act_ref[:, c * tf:(c + 1) * tf] = (_silu(g) * u).astype(BF16)
    out = _dot(act_ref[...], wout_ref[...])
    o_ref[...] = x_ref[...] + (0.5 * gate) * _rms(out, gpost_ref[...])


def _ffn(x, mod_l, g_pre, g_post, w_in, w_out, rows, j):
    m, d = x.shape
    dff = w_out.shape[0]
    tm = rows.tile
    tf = 256
    assert dff % tf == 0
    return pl.pallas_call(
        functools.partial(_ffn_kernel, j=j, tf=tf),
        out_shape=jax.ShapeDtypeStruct((m, d), F32),
        grid=(rows.n_tiles,),
        in_specs=[pl.BlockSpec((tm, d), lambda i: (i, 0)),
                  rows.mod_spec(mod_l.shape[-1]),
                  pl.BlockSpec((1, d), lambda i: (0, 0)),
                  pl.BlockSpec((1, d), lambda i: (0, 0)),
                  pl.BlockSpec((d, 2 * dff), lambda i: (0, 0), pipeline_mode=pl.Buffered(1)),
                  pl.BlockSpec((dff, d), lambda i: (0, 0), pipeline_mode=pl.Buffered(1))],
        out_specs=pl.BlockSpec((tm, d), lambda i: (i, 0)),
        scratch_shapes=[pltpu.VMEM((tm, d), BF16), pltpu.VMEM((tm, dff), BF16)],
        compiler_params=_params("parallel"),
        name="swiglu_sublayer",
    )(x, mod_l, g_pre, g_post, w_in, w_out)


def _inproj_kernel(x_ref, mod_ref, gpre_ref, w_ref, o_ref, h_ref, *, tn):
    d = x_ref.shape[1]
    shift, scale, _ = _mod_parts(mod_ref[...], 1, d)
    h_ref[...] = (_rms(x_ref[...], gpre_ref[...]) * (1.0 + scale) + shift).astype(BF16)
    for c in range(w_ref.shape[1] // tn):
        o_ref[:, c * tn:(c + 1) * tn] = _dot(h_ref[...], w_ref[:, c * tn:(c + 1) * tn])


def _inproj(x, mod_l, g_pre, w, rows):
    m, d = x.shape
    n = w.shape[1]
    tm = rows.tile
    tn = 512
    assert n % tn == 0
    return pl.pallas_call(
        functools.partial(_inproj_kernel, tn=tn),
        out_shape=jax.ShapeDtypeStruct((m, n), F32),
        grid=(rows.n_tiles,),
        in_specs=[pl.BlockSpec((tm, d), lambda i: (i, 0)),
                  rows.mod_spec(mod_l.shape[-1]),
                  pl.BlockSpec((1, d), lambda i: (0, 0)),
                  pl.BlockSpec((d, n), lambda i: (0, 0), pipeline_mode=pl.Buffered(1))],
        out_specs=pl.BlockSpec((tm, n), lambda i: (i, 0)),
        scratch_shapes=[pltpu.VMEM((tm, d), BF16)],
        compiler_params=_params("parallel"),
        name="mixer_in_proj",
    )(x, mod_l, g_pre, w)


def _outproj_kernel(x_ref, a_ref, b_ref, mod_ref, gpost_ref, w_ref, o_ref):
    d = x_ref.shape[1]
    da = a_ref.shape[1]
    _, _, gate = _mod_parts(mod_ref[...], 1, d)
    out = (_dot(a_ref[...].astype(BF16), w_ref[:da, :])
           + _dot(b_ref[...].astype(BF16), w_ref[da:, :]))
    o_ref[...] = x_ref[...] + gate * _rms(out, gpost_ref[...])


def _outproj(x, o_a, o_b, mod_l, g_post, w, rows):
    m, d = x.shape
    tm = rows.tile
    return pl.pallas_call(
        _outproj_kernel,
        out_shape=jax.ShapeDtypeStruct((m, d), F32),
        grid=(rows.n_tiles,),
        in_specs=[pl.BlockSpec((tm, d), lambda i: (i, 0)),
                  pl.BlockSpec((tm, o_a.shape[1]), lambda i: (i, 0)),
                  pl.BlockSpec((tm, o_b.shape[1]), lambda i: (i, 0)),
                  rows.mod_spec(mod_l.shape[-1]),
                  pl.BlockSpec((1, d), lambda i: (0, 0)),
                  pl.BlockSpec(w.shape, lambda i: (0, 0), pipeline_mode=pl.Buffered(1))],
        out_specs=pl.BlockSpec((tm, d), lambda i: (i, 0)),
        compiler_params=_params("parallel"),
        name="mixer_out_proj",
    )(x, o_a, o_b, mod_l, g_post, w)


def _conv_kernel(x_ref, xp_ref, xn_ref, mod_ref, gpre_ref, gpost_ref, win_ref, cw_ref, wout_ref, o_ref,
                 h_ref, u_ref, z_ref, *, tc, rows):
    tm, d = x_ref.shape
    i = pl.program_id(0)
    shift, scale, gate = _mod_parts(mod_ref[...], 1, d)
    g_pre = gpre_ref[...]

    def pre(x):
        return (_rms(x, g_pre) * (1.0 + scale) + shift).astype(BF16)

    h_ref[0:HALO, :] = pre(xp_ref[...])
    h_ref[HALO:HALO + tm, :] = pre(x_ref[...])
    h_ref[HALO + tm:, :] = pre(xn_ref[...])

    seq_len = jnp.where(i < rows.n_ctx_tiles, rows.ctx_len, rows.lat_len)
    pos = (i * tm + lax.broadcasted_iota(jnp.int32, (tm, 1), 0)) & (seq_len - 1)
    has_left = pos != 0
    has_right = pos != seq_len - 1

    for c in range(d // tc):
        cols = slice(c * tc, (c + 1) * tc)
        b_gate = _dot(h_ref[HALO:HALO + tm, :], win_ref[:, c * tc:(c + 1) * tc])
        c_gate = _dot(h_ref[...], win_ref[:, d + c * tc:d + (c + 1) * tc])
        x_in = _dot(h_ref[...], win_ref[:, 2 * d + c * tc:2 * d + (c + 1) * tc])
        u_ref[...] = c_gate * x_in
        cw = cw_ref[:, cols]
        y = (cw[0:1, :] * jnp.where(has_left, u_ref[HALO - 1:HALO - 1 + tm, :], 0.0)
             + cw[1:2, :] * u_ref[HALO:HALO + tm, :]
             + cw[2:3, :] * jnp.where(has_right, u_ref[HALO + 1:HALO + 1 + tm, :], 0.0))
        z_ref[:, cols] = (b_gate * y).astype(BF16)
    out = _dot(z_ref[...], wout_ref[...])
    o_ref[...] = x_ref[...] + gate * _rms(out, gpost_ref[...])


def _conv_mixer(x, mod_l, g_pre, g_post, w_in, conv_w, w_out, rows):
    m, d = x.shape
    tm = rows.tile
    tc = 256
    assert rows.ctx_len & (rows.ctx_len - 1) == 0 and rows.lat_len & (rows.lat_len - 1) == 0
    hb = tm // HALO
    last = m // HALO - 1
    return pl.pallas_call(
        functools.partial(_conv_kernel, tc=tc, rows=rows),
        out_shape=jax.ShapeDtypeStruct((m, d), F32),
        grid=(rows.n_tiles,),
        in_specs=[pl.BlockSpec((tm, d), lambda i: (i, 0)),
                  pl.BlockSpec((HALO, d), lambda i: (jnp.maximum(i * hb - 1, 0), 0)),
                  pl.BlockSpec((HALO, d), lambda i: (jnp.minimum((i + 1) * hb, last), 0)),
                  rows.mod_spec(mod_l.shape[-1]),
                  pl.BlockSpec((1, d), lambda i: (0, 0)),
                  pl.BlockSpec((1, d), lambda i: (0, 0)),
                  pl.BlockSpec((d, 3 * d), lambda i: (0, 0), pipeline_mode=pl.Buffered(1)),
                  pl.BlockSpec((CONV_W, d), lambda i: (0, 0)),
                  pl.BlockSpec((d, d), lambda i: (0, 0), pipeline_mode=pl.Buffered(1))],
        out_specs=pl.BlockSpec((tm, d), lambda i: (i, 0)),
        scratch_shapes=[pltpu.VMEM((tm + 2 * HALO, d), BF16),
                        pltpu.VMEM((tm + 2 * HALO, tc), F32),
                        pltpu.VMEM((tm, d), BF16)],
        compiler_params=_params("parallel"),
        name="conv_mixer",
    )(x, x, x, mod_l, g_pre, g_post, w_in, conv_w, w_out)


def _pair_codes(c, forward):
    t = lax.broadcasted_iota(jnp.int32, (c, c), 0)
    s = lax.broadcasted_iota(jnp.int32, (c, c), 1)
    x = t ^ s
    n_levels = int(math.log2(c // SCAN_GROUP))
    code = jnp.full((c, c), n_levels, jnp.int32)
    for lvl in range(n_levels - 1, 0, -1):
        code = jnp.where(x >= (c >> lvl), lvl, code)
    code = jnp.where(x < SCAN_GROUP, n_levels + 1, code)
    visited = (s <= t) if forward else (s >= t)
    return jnp.where(visited, code, 0), n_levels


def _scan_chunk(q, z, v, lb, st_ref, c_ref, code, n_levels, forward):
    c_len, dk = q.shape
    row = lax.broadcasted_iota(jnp.int32, (c_len, dk), 0)

    e_abs = jnp.exp(-jnp.abs(z))
    inv = 1.0 / (1.0 + e_abs)
    sig_big, sig_small = inv, e_abs * inv
    sig = jnp.where(z >= 0, sig_big, sig_small)
    sig_neg = jnp.where(z >= 0, sig_small, sig_big)
    f = lb + (1.0 - lb) * sig
    log_sig = jnp.minimum(z, 0.0) - jnp.log1p(e_abs)
    log_f = jnp.maximum(jnp.log1p(-lb) + log_sig, jnp.log(jnp.maximum(f, 1e-30)))
    k = (1.0 - lb) * sig_neg

    c = log_f
    step = 1
    while step < c_len:
        if forward:
            c = c + jnp.where(row >= step, pltpu.roll(c, step, 0), 0.0)
        else:
            c = c + jnp.where(row < c_len - step, pltpu.roll(c, c_len - step, 0), 0.0)
        step *= 2
    c_ref[...] = c

    def reference_rows(block, offset):
        parts = [jnp.broadcast_to(c_ref[n * block + offset:n * block + offset + 1, :], (block, dk))
                 for n in range(c_len // block)]
        return parts[0] if len(parts) == 1 else jnp.concatenate(parts, axis=0)

    att = jnp.zeros((c_len, c_len), F32)
    for lvl in range(1, n_levels + 1):
        half = c_len >> lvl
        e = jnp.exp(-jnp.abs(c - reference_rows(2 * half, half)))
        att = jnp.where(code == lvl, _dot_nt((q * e).astype(BF16), (k * e).astype(BF16)), att)

    dq = c - reference_rows(SCAN_GROUP, 0 if forward else SCAN_GROUP - 1)
    safe = jnp.max(-dq) <= SAFE_EXP

    def factorised():
        eq = jnp.exp(dq)
        ek = jnp.exp(jnp.minimum(-dq, SAFE_EXP))
        return _dot_nt((q * eq).astype(BF16), (k * ek).astype(BF16))

    def pairwise():
        t = lax.broadcasted_iota(jnp.int32, (c_len, c_len), 0)
        s = lax.broadcasted_iota(jnp.int32, (c_len, c_len), 1)
        pos = row & (SCAN_GROUP - 1)
        acc = jnp.zeros((c_len, c_len), F32)
        for dist in range(SCAN_GROUP):
            if dist == 0:
                k_s, c_s = k, c
            else:
                shift = dist if forward else c_len - dist
                k_s, c_s = pltpu.roll(k, shift, 0), pltpu.roll(c, shift, 0)
            valid = (pos >= dist) if forward else (pos + dist < SCAN_GROUP)
            term = q * k_s * jnp.exp(jnp.where(valid, c - c_s, 0.0))
            a = jnp.sum(jnp.where(valid, term, 0.0), axis=-1, keepdims=True)
            acc = jnp.where(s == (t - dist if forward else t + dist), a, acc)
        return acc

    att = jnp.where(code == n_levels + 1, lax.cond(safe, factorised, pairwise), att)

    st = st_ref[...]
    o = _dot(att.astype(BF16), v.astype(BF16)) + _dot_nt((q * jnp.exp(c)).astype(BF16), st.astype(BF16))
    end = c_len - 1 if forward else 0
    c_end = c_ref[end:end + 1, :]
    k_dec = (k * jnp.exp(c_end - c)).astype(BF16)
    st_ref[...] = st * jnp.exp(c_end) + _dot_tn(v.astype(BF16), k_dec)
    return o


def _hgrn_kernel(q_ref, zf_ref, zb_ref, v_ref, g_ref, alb_ref, gn_ref, s0_ref, o_ref, s_ref,
                 of_ref, ob_ref, c_ref, *, layer_j, chunk):
    seq = q_ref.shape[0]
    n_chunks = seq // chunk

    a = alb_ref[...]
    e = jnp.exp(a - jnp.max(a, axis=0, keepdims=True))
    p = e / jnp.sum(e, axis=0, keepdims=True)
    cum = p[0]
    for n in range(1, layer_j + 1):
        cum = cum + p[n]
    lb = jnp.clip(cum - p[0], 0.0, LB_MAX)

    s_ref[...] = s0_ref[...]
    code_f, n_levels = _pair_codes(chunk, True)
    code_b, _ = _pair_codes(chunk, False)

    def body(n, carry):
        rf = pl.ds(pl.multiple_of(n * chunk, chunk), chunk)
        rb = pl.ds(pl.multiple_of((n_chunks - 1 - n) * chunk, chunk), chunk)
        of_ref[rf, :] = _scan_chunk(q_ref[rf, :], zf_ref[rf, :], v_ref[rf, :], lb[0:1], s_ref.at[0],
                                    c_ref.at[0], code_f, n_levels, True)
        ob_ref[rb, :] = _scan_chunk(q_ref[rb, :], zb_ref[rb, :], v_ref[rb, :], lb[1:2], s_ref.at[1],
                                    c_ref.at[1], code_b, n_levels, False)
        return carry

    lax.fori_loop(0, n_chunks, body, 0)

    o = of_ref[...] + ob_ref[...]
    o_ref[...] = _rms(o, gn_ref[...]) * _silu(g_ref[...])


def _hgrn(proj, row_block0, seq, n_batch, a_lb, gnorm, s0_t, layer_j):
    dk = LANES
    heads = A_HEADS
    chunk = min(SCAN_CHUNK, seq)

    def col(section):
        return pl.BlockSpec((seq, dk), lambda b, h: (row_block0 + b, section * heads + h))

    return pl.pallas_call(
        functools.partial(_hgrn_kernel, layer_j=layer_j, chunk=chunk),
        out_shape=(jax.ShapeDtypeStruct((n_batch * seq, heads * dk), F32),
                   jax.ShapeDtypeStruct(s0_t.shape, F32)),
        grid=(n_batch, heads),
        in_specs=[col(0), col(1), col(2), col(3), col(4),
                  pl.BlockSpec((a_lb.shape[0], 2, dk), lambda b, h: (0, 0, h)),
                  pl.BlockSpec((1, dk), lambda b, h: (0, 0)),
                  pl.BlockSpec((None, 2, None, dk, dk), lambda b, h: (b, 0, h, 0, 0))],
        out_specs=(pl.BlockSpec((seq, dk), lambda b, h: (b, h)),
                   pl.BlockSpec((None, 2, None, dk, dk), lambda b, h: (b, 0, h, 0, 0))),
        scratch_shapes=[pltpu.VMEM((seq, dk), F32), pltpu.VMEM((seq, dk), F32),
                        pltpu.VMEM((2, chunk, dk), F32)],
        compiler_params=_params("parallel", "parallel"),
        name="hgrn2_scan",
    )(proj, proj, proj, proj, proj, a_lb, gnorm, s0_t)


def _head_lanes(shape, head):
    lane = lax.broadcasted_iota(jnp.int32, shape, len(shape) - 1)
    half = shape[-1] // 2
    return (lane < half) if head == 0 else (lane >= half)


def _ctx_attn_kernel(q_ref, k_ref, v_ref, o_ref, *, scale):
    k = k_ref[...].astype(BF16)
    v = v_ref[...].astype(BF16)
    q = q_ref[...]
    outs = []
    for head in range(2):
        qh = jnp.where(_head_lanes(q.shape, head), q, 0.0).astype(BF16)
        s = _dot_nt(qh, k) * scale
        p = jnp.exp(s - jnp.max(s, axis=-1, keepdims=True))
        outs.append(_dot(p.astype(BF16), v) / jnp.sum(p, axis=-1, keepdims=True))
    o_ref[...] = jnp.where(_head_lanes(q.shape, 0), outs[0], outs[1])


def _ctx_attention(proj, n_batch, seq, col0):
    pairs = B_HEADS // 2
    cb = col0 // LANES

    def col(section):
        return pl.BlockSpec((seq, LANES), lambda b, h: (b, cb + section * pairs + h))

    return pl.pallas_call(
        functools.partial(_ctx_attn_kernel, scale=1.0 / math.sqrt(LANES // 2)),
        out_shape=jax.ShapeDtypeStruct((n_batch * seq, pairs * LANES), F32),
        grid=(n_batch, pairs),
        in_specs=[col(0), col(1), col(2)],
        out_specs=pl.BlockSpec((seq, LANES), lambda b, h: (b, h)),
        compiler_params=_params("parallel", "parallel"),
        name="context_attention",
    )(proj, proj, proj)


def _rpb_tiles_kernel(rpb_ref, o_ref):
    h = pl.program_id(0)
    qc = lax.broadcasted_iota(jnp.int32, (GRID_W, GRID_W), 0)
    kc = lax.broadcasted_iota(jnp.int32, (GRID_W, GRID_W), 1)
    c0 = jnp.clip(qc - NA_COLS // 2, 0, GRID_W - NA_COLS)
    col_ok = (kc >= c0) & (kc < c0 + NA_COLS)
    dc = jnp.clip(kc - qc, -(NA_COLS - 1), NA_COLS - 1) + NA_COLS - 1
    for dr in range(2 * NA_ROWS - 1):
        t = jnp.zeros((GRID_W, GRID_W), F32)
        for i in range(2 * NA_COLS - 1):
            t = jnp.where(dc == i, rpb_ref[h, dr, i], t)
        o_ref[dr] = jnp.where(col_ok, t, MASKED)


def _rpb_tiles(rpb):
    heads, n_dr, n_dc = rpb.shape
    return pl.pallas_call(
        _rpb_tiles_kernel,
        out_shape=jax.ShapeDtypeStruct((heads, n_dr, GRID_W, GRID_W), F32),
        grid=(heads,),
        in_specs=[pl.BlockSpec(memory_space=pltpu.SMEM)],
        out_specs=pl.BlockSpec((None, n_dr, GRID_W, GRID_W), lambda h: (h, 0, 0, 0)),
        compiler_params=_params("parallel"),
        name="rpb_tiles",
    )(rpb)


def _nbr_geometry(grid_rows):
    win = min(NA_ROWS, grid_rows)
    blocks = []
    for qb in range(grid_rows // Q_ROWS):
        r0 = qb * Q_ROWS
        key0 = min(max(r0 - win // 2, 0), grid_rows - KEY_ROWS)
        tiles = []
        for i in range(Q_ROWS):
            r = r0 + i
            start = min(max(r - win // 2, 0), grid_rows - win)
            assert key0 <= start and start + win <= key0 + KEY_ROWS
            tiles.append([(key0 + jr - r + NA_ROWS - 1) if start <= key0 + jr < start + win else None
                          for jr in range(KEY_ROWS)])
        blocks.append((key0, tiles))
    return blocks


def _nbr_attn_kernel(q_ref, k_ref, v_ref, ck_ref, cv_ref, tiles_ref, o_ref, bias_ref, *, scale, geometry):
    w = GRID_W
    nq = Q_ROWS * w
    nk = KEY_ROWS * w

    @pl.when(pl.program_id(1) == 0)
    def _():
        masked = jnp.full((w, w), MASKED, F32)
        for head in range(2):
            for qb, (_, tiles) in enumerate(geometry):
                for i, row_tiles in enumerate(tiles):
                    for jr, dr in enumerate(row_tiles):
                        tile = masked if dr is None else tiles_ref[head, dr]
                        bias_ref[head, qb, i * w:(i + 1) * w, jr * w:(jr + 1) * w] = tile

    ck = ck_ref[...].astype(BF16)
    cv = cv_ref[...].astype(BF16)
    for qb, (key0, _) in enumerate(geometry):
        q = q_ref[qb * nq:(qb + 1) * nq, :]
        k = k_ref[key0 * w:key0 * w + nk, :].astype(BF16)
        v = v_ref[key0 * w:key0 * w + nk, :].astype(BF16)
        outs = []
        for head in range(2):
            qh = jnp.where(_head_lanes(q.shape, head), q, 0.0).astype(BF16)
            s_loc = _dot_nt(qh, k) * scale + bias_ref[head, qb]
            s_ctx = _dot_nt(qh, ck) * scale
            m = jnp.maximum(jnp.max(s_loc, axis=-1, keepdims=True), jnp.max(s_ctx, axis=-1, keepdims=True))
            p_loc = jnp.exp(s_loc - m)
            p_ctx = jnp.exp(s_ctx - m)
            denom = jnp.sum(p_loc, axis=-1, keepdims=True) + jnp.sum(p_ctx, axis=-1, keepdims=True)
            outs.append((_dot(p_loc.astype(BF16), v) + _dot(p_ctx.astype(BF16), cv)) / denom)
        o_ref[qb * nq:(qb + 1) * nq, :] = jnp.where(_head_lanes(q.shape, 0), outs[0], outs[1])


def _nbr_attention(proj, row_block0, seq, n_batch, col0, ck, cv, tiles):
    pairs = B_HEADS // 2
    cb = col0 // LANES
    past = ck.shape[2]
    grid_rows = seq // GRID_W
    assert grid_rows % Q_ROWS == 0 and grid_rows >= KEY_ROWS
    geometry = _nbr_geometry(grid_rows)

    def col(section):
        return pl.BlockSpec((seq, LANES), lambda h, b: (row_block0 + b, cb + section * pairs + h))

    return pl.pallas_call(
        functools.partial(_nbr_attn_kernel, scale=1.0 / math.sqrt(LANES // 2), geometry=geometry),
        out_shape=jax.ShapeDtypeStruct((n_batch * seq, pairs * LANES), F32),
        grid=(pairs, n_batch),
        in_specs=[col(0), col(1), col(2),
                  pl.BlockSpec((None, None, past, LANES), lambda h, b: (b, h, 0, 0)),
                  pl.BlockSpec((None, None, past, LANES), lambda h, b: (b, h, 0, 0)),
                  pl.BlockSpec((2,) + tiles.shape[1:], lambda h, b: (h, 0, 0, 0))],
        out_specs=pl.BlockSpec((seq, LANES), lambda h, b: (b, h)),
        scratch_shapes=[pltpu.VMEM((2, len(geometry), Q_ROWS * GRID_W, KEY_ROWS * GRID_W), F32)],
        compiler_params=_params("arbitrary", "arbitrary"),
        name="neighbourhood_attention",
    )(proj, proj, proj, ck, cv, tiles)


def _head_pairs(t):
    b, h, l, dh = t.shape
    return t.reshape(b, h // 2, 2, l, dh).transpose(0, 1, 3, 2, 4).reshape(b, h // 2, l, 2 * dh)


def _heads_major(t, n_batch, seq):
    return t.reshape(n_batch, seq, B_HEADS, -1).transpose(0, 2, 1, 3)


def kernel(x_prompt, x_sample, c, state_hgrn, cache_k, cache_v, c_ctx, mod_w, mod_b, norm_pre, norm_post,
           ffn_w_in, ffn_w_out, ab_w_in, ab_w_out, a_lb, a_gnorm, b_rpb, cv_w_in, cv_conv, cv_w_out):
    n_ctx, ctx_len, d = x_prompt.shape
    n_lat, lat_len, _ = x_sample.shape
    depth = mod_w.shape[0]
    d_a = a_lb.shape[-1]
    rows = _Rows(n_ctx * ctx_len, ctx_len, n_lat * lat_len, lat_len, ROW_TILE)
    assert (n_ctx * ctx_len) % lat_len == 0
    lat_block0 = (n_ctx * ctx_len) // lat_len

    x = jnp.concatenate([x_prompt.reshape(-1, d), x_sample.reshape(-1, d)], axis=0)

    cond = jnp.concatenate([c_ctx[None, :], c], axis=0)
    cond = jnp.pad(cond, ((0, -cond.shape[0] % 8), (0, 0)))
    mod = _modulation(cond, mod_w, mod_b)
    mod = mod.reshape(depth, cond.shape[0], 1, N_MOD * d)

    ffn_w_in = ffn_w_in.astype(BF16)
    ffn_w_out = ffn_w_out.astype(BF16)
    ab_w_in = ab_w_in.astype(BF16)
    ab_w_out = ab_w_out.astype(BF16)
    cv_w_in = cv_w_in.astype(BF16)
    cv_w_out = cv_w_out.astype(BF16)

    new_s, new_k, new_v = [], [], []
    for l in range(depth):
        j = l // 2

        def gains(t, sub):
            return t[l, sub][None, :]

        x = _ffn(x, mod[l], gains(norm_pre, 0), gains(norm_post, 0), ffn_w_in[l, 0], ffn_w_out[l, 0], rows, 0)
        if l % 2 == 0:
            proj = _inproj(x, mod[l], gains(norm_pre, 1), ab_w_in[j], rows)
            gn = a_gnorm[j][None, :]
            zeros = jnp.zeros((n_ctx, 2, A_HEADS, LANES, LANES), F32)
            oa_ctx, s_ctx = _hgrn(proj, 0, ctx_len, n_ctx, a_lb, gn, zeros, j)
            s_lat = jnp.swapaxes(state_hgrn[:, j].astype(F32), -1, -2)
            oa_lat, _ = _hgrn(proj, lat_block0, lat_len, n_lat, a_lb, gn, s_lat, j)
            ob_ctx = _ctx_attention(proj, n_ctx, ctx_len, 5 * d_a)
            ob_lat = _nbr_attention(proj, lat_block0, lat_len, n_lat, 5 * d_a,
                                    _head_pairs(cache_k[:, j]), _head_pairs(cache_v[:, j]),
                                    _rpb_tiles(b_rpb[j]))
            d_b = ob_ctx.shape[1]
            k_cols = proj[:n_ctx * ctx_len, 5 * d_a + d_b:5 * d_a + 2 * d_b]
            v_cols = proj[:n_ctx * ctx_len, 5 * d_a + 2 * d_b:5 * d_a + 3 * d_b]
            new_s.append(jnp.swapaxes(s_ctx, -1, -2))
            new_k.append(_heads_major(k_cols, n_ctx, ctx_len))
            new_v.append(_heads_major(v_cols, n_ctx, ctx_len))
            x = _outproj(x, jnp.concatenate([oa_ctx, oa_lat], axis=0), jnp.concatenate([ob_ctx, ob_lat], axis=0),
                         mod[l], gains(norm_post, 1), ab_w_out[j], rows)
        else:
            x = _conv_mixer(x, mod[l], gains(norm_pre, 1), gains(norm_post, 1), cv_w_in[j], cv_conv[j],
                            cv_w_out[j], rows)
        x = _ffn(x, mod[l], gains(norm_pre, 2), gains(norm_post, 2), ffn_w_in[l, 1], ffn_w_out[l, 1], rows, 2)

    n_ctx_rows = n_ctx * ctx_len
    return (x[:n_ctx_rows].reshape(x_prompt.shape), x[n_ctx_rows:].reshape(x_sample.shape),
            jnp.stack(new_s, axis=1), jnp.stack(new_k, axis=1), jnp.stack(new_v, axis=1))
```

```python
import functools
import math

import jax
import jax.numpy as jnp
from jax import lax
from jax.experimental import pallas as pl
from jax.experimental.pallas import tpu as pltpu

F32 = jnp.float32
BF16 = jnp.bfloat16

EPS = 1e-6
LB_MAX = 1.0 - 1e-4
MASKED = -1e30

GRID_W = 64
A_HEADS = 4
B_HEADS = 8
NA_ROWS = 8
NA_COLS = 16
CONV_W = 3
N_MOD = 9

VMEM_LIMIT_BYTES = 56 * 1024 * 1024
LANES = 128

ROW_TILE = 512
HALO = 16
SCAN_CHUNK = 128
SCAN_GROUP = 16
SAFE_EXP = 80.0
Q_ROWS = 8
KEY_ROWS = 16


def _params(*sem):
    return pltpu.CompilerParams(dimension_semantics=sem, vmem_limit_bytes=VMEM_LIMIT_BYTES)


def _rms(x, g):
    return x * lax.rsqrt(jnp.mean(x * x, axis=-1, keepdims=True) + EPS) * g


def _silu(x):
    return x * jax.nn.sigmoid(x)


def _dot(a, b):
    return jnp.dot(a, b, preferred_element_type=F32)


def _dot_nt(a, b):
    return lax.dot_general(a, b, (((1,), (1,)), ((), ())), preferred_element_type=F32)


def _dot_tn(a, b):
    return lax.dot_general(a, b, (((0,), (0,)), ((), ())), preferred_element_type=F32)


def _mod_parts(mod, j, d):
    return (mod[:, (3 * j) * d:(3 * j + 1) * d],
            mod[:, (3 * j + 1) * d:(3 * j + 2) * d],
            mod[:, (3 * j + 2) * d:(3 * j + 3) * d])


class _Rows:
    def __init__(self, n_ctx_rows, ctx_len, n_lat_rows, lat_len, tile):
        assert ctx_len % tile == 0 or tile % ctx_len == 0
        assert lat_len % tile == 0 and n_ctx_rows % tile == 0 and n_lat_rows % tile == 0
        self.tile = tile
        self.n_ctx_tiles = n_ctx_rows // tile
        self.n_tiles = (n_ctx_rows + n_lat_rows) // tile
        self.lat_tiles_per_batch = lat_len // tile
        self.ctx_len = ctx_len
        self.lat_len = lat_len

    def mod_row(self, i):
        return jnp.where(i < self.n_ctx_tiles, 0, 1 + (i - self.n_ctx_tiles) // self.lat_tiles_per_batch)

    def group_specs(self, width):
        nc = self.n_ctx_tiles
        return [pl.BlockSpec((self.tile, width), lambda i: (jnp.minimum(i, nc - 1), 0)),
                pl.BlockSpec((self.tile, width), lambda i: (jnp.maximum(i - nc, 0), 0))]

    def mod_spec(self, width):
        return pl.BlockSpec((None, 1, width), lambda i: (self.mod_row(i), 0, 0))


def _mod_kernel(cond_ref, w_ref, b_ref, o_ref):
    s = _silu(cond_ref[...]).astype(BF16)
    o_ref[...] = _dot(s, w_ref[...].astype(BF16)) + b_ref[...]


def _modulation(cond, mod_w, mod_b):
    depth, d, n = mod_w.shape
    rows = cond.shape[0]
    tn = 1536
    assert n % tn == 0
    return pl.pallas_call(
        _mod_kernel,
        out_shape=jax.ShapeDtypeStruct((depth, rows, n), F32),
        grid=(depth, n // tn),
        in_specs=[pl.BlockSpec((rows, d), lambda l, j: (0, 0)),
                  pl.BlockSpec((None, d, tn), lambda l, j: (l, 0, j)),
                  pl.BlockSpec((None, 1, tn), lambda l, j: (l, 0, j))],
        out_specs=pl.BlockSpec((None, rows, tn), lambda l, j: (l, 0, j)),
        compiler_params=_params("parallel", "parallel"),
        name="adaln_mod",
    )(cond, mod_w, mod_b.reshape(depth, 1, n))


def _ffn_kernel(*refs, j, tf, n_ctx_tiles, split_in, split_out):
    n_x = 2 if split_in else 1
    x_refs = refs[:n_x]
    mod_ref, gpre_ref, gpost_ref, win_ref, wout_ref = refs[n_x:n_x + 5]
    o_refs = refs[n_x + 5:-2]
    h_ref, act_ref = refs[-2:]
    is_ctx = pl.program_id(0) < n_ctx_tiles

    def load_x():
        if split_in:
            return jnp.where(is_ctx, x_refs[0][...], x_refs[1][...])
        return x_refs[0][...]

    d = x_refs[0].shape[1]
    dff = wout_ref.shape[0]
    shift, scale, gate = _mod_parts(mod_ref[...], j, d)
    h_ref[...] = (_rms(load_x(), gpre_ref[...]) * (1.0 + scale) + shift).astype(BF16)
    for c in range(dff // tf):
        g = _dot(h_ref[...], win_ref[:, c * tf:(c + 1) * tf])
        u = _dot(h_ref[...], win_ref[:, dff + c * tf:dff + (c + 1) * tf])
        act_ref[:, c * tf:(c + 1) * tf] = (_silu(g) * u).astype(BF16)
    out = _dot(act_ref[...], wout_ref[...])
    y = load_x() + (0.5 * gate) * _rms(out, gpost_ref[...])
    if split_out:
        @pl.when(is_ctx)
        def _():
            o_refs[0][...] = y

        @pl.when(jnp.logical_not(is_ctx))
        def _():
            o_refs[1][...] = y
    else:
        o_refs[0][...] = y


def _ffn(x, mod_l, g_pre, g_post, w_in, w_out, rows, j, split_out=False):
    split_in = isinstance(x, tuple)
    d = x[0].shape[1] if split_in else x.shape[1]
    dff = w_out.shape[0]
    tm = rows.tile
    tf = 256
    assert dff % tf == 0
    out_shape = jax.ShapeDtypeStruct((rows.n_tiles * tm, d), F32)
    out_spec = pl.BlockSpec((tm, d), lambda i: (i, 0))
    if split_out:
        out_shape = (jax.ShapeDtypeStruct((rows.n_ctx_tiles * tm, d), F32),
                     jax.ShapeDtypeStruct(((rows.n_tiles - rows.n_ctx_tiles) * tm, d), F32))
        out_spec = tuple(rows.group_specs(d))
    return pl.pallas_call(
        functools.partial(_ffn_kernel, j=j, tf=tf, n_ctx_tiles=rows.n_ctx_tiles, split_in=split_in,
                          split_out=split_out),
        out_shape=out_shape,
        grid=(rows.n_tiles,),
        in_specs=(rows.group_specs(d) if split_in else [pl.BlockSpec((tm, d), lambda i: (i, 0))]) + [
                  rows.mod_spec(mod_l.shape[-1]),
                  pl.BlockSpec((1, d), lambda i: (0, 0)),
                  pl.BlockSpec((1, d), lambda i: (0, 0)),
                  pl.BlockSpec((d, 2 * dff), lambda i: (0, 0), pipeline_mode=pl.Buffered(1)),
                  pl.BlockSpec((dff, d), lambda i: (0, 0), pipeline_mode=pl.Buffered(1))],
        out_specs=out_spec,
        scratch_shapes=[pltpu.VMEM((tm, d), BF16), pltpu.VMEM((tm, dff), BF16)],
        compiler_params=_params("arbitrary" if split_out else "parallel"),
        name="swiglu_sublayer",
    )(*(x if split_in else (x,)), mod_l, g_pre, g_post, w_in, w_out)


def _inproj_kernel(x_ref, mod_ref, gpre_ref, w_ref, o_ref, h_ref, *, tn):
    d = x_ref.shape[1]
    shift, scale, _ = _mod_parts(mod_ref[...], 1, d)
    h_ref[...] = (_rms(x_ref[...], gpre_ref[...]) * (1.0 + scale) + shift).astype(BF16)
    for c in range(w_ref.shape[1] // tn):
        o_ref[:, c * tn:(c + 1) * tn] = _dot(h_ref[...], w_ref[:, c * tn:(c + 1) * tn])


def _inproj(x, mod_l, g_pre, w, rows):
    m, d = x.shape
    n = w.shape[1]
    tm = rows.tile
    tn = 512
    assert n % tn == 0
    return pl.pallas_call(
        functools.partial(_inproj_kernel, tn=tn),
        out_shape=jax.ShapeDtypeStruct((m, n), F32),
        grid=(rows.n_tiles,),
        in_specs=[pl.BlockSpec((tm, d), lambda i: (i, 0)),
                  rows.mod_spec(mod_l.shape[-1]),
                  pl.BlockSpec((1, d), lambda i: (0, 0)),
                  pl.BlockSpec((d, n), lambda i: (0, 0), pipeline_mode=pl.Buffered(1))],
        out_specs=pl.BlockSpec((tm, n), lambda i: (i, 0)),
        scratch_shapes=[pltpu.VMEM((tm, d), BF16)],
        compiler_params=_params("parallel"),
        name="mixer_in_proj",
    )(x, mod_l, g_pre, w)


def _outproj_kernel(x_ref, ac_ref, al_ref, bc_ref, bl_ref, mod_ref, gpost_ref, w_ref, o_ref, *, n_ctx_tiles):
    d = x_ref.shape[1]
    da = ac_ref.shape[1]
    is_ctx = pl.program_id(0) < n_ctx_tiles
    _, _, gate = _mod_parts(mod_ref[...], 1, d)
    a = jnp.where(is_ctx, ac_ref[...], al_ref[...]).astype(BF16)
    b = jnp.where(is_ctx, bc_ref[...], bl_ref[...]).astype(BF16)
    out = _dot(a, w_ref[:da, :]) + _dot(b, w_ref[da:, :])
    o_ref[...] = x_ref[...] + gate * _rms(out, gpost_ref[...])


def _outproj(x, o_a, o_b, mod_l, g_post, w, rows):
    m, d = x.shape
    tm = rows.tile
    return pl.pallas_call(
        functools.partial(_outproj_kernel, n_ctx_tiles=rows.n_ctx_tiles),
        out_shape=jax.ShapeDtypeStruct((m, d), F32),
        grid=(rows.n_tiles,),
        in_specs=[pl.BlockSpec((tm, d), lambda i: (i, 0))]
                 + rows.group_specs(o_a[0].shape[1]) + rows.group_specs(o_b[0].shape[1]) + [
                  rows.mod_spec(mod_l.shape[-1]),
                  pl.BlockSpec((1, d), lambda i: (0, 0)),
                  pl.BlockSpec(w.shape, lambda i: (0, 0), pipeline_mode=pl.Buffered(1))],
        out_specs=pl.BlockSpec((tm, d), lambda i: (i, 0)),
        compiler_params=_params("parallel"),
        name="mixer_out_proj",
    )(x, *o_a, *o_b, mod_l, g_post, w)


def _conv_kernel(x_ref, xp_ref, xn_ref, mod_ref, gpre_ref, gpost_ref, win_ref, cw_ref, wout_ref, o_ref,
                 h_ref, u_ref, z_ref, *, tc, rows):
    tm, d = x_ref.shape
    i = pl.program_id(0)
    shift, scale, gate = _mod_parts(mod_ref[...], 1, d)
    g_pre = gpre_ref[...]

    def pre(x):
        return (_rms(x, g_pre) * (1.0 + scale) + shift).astype(BF16)

    h_ref[0:HALO, :] = pre(xp_ref[...])
    h_ref[HALO:HALO + tm, :] = pre(x_ref[...])
    h_ref[HALO + tm:, :] = pre(xn_ref[...])

    seq_len = jnp.where(i < rows.n_ctx_tiles, rows.ctx_len, rows.lat_len)
    pos = (i * tm + lax.broadcasted_iota(jnp.int32, (tm, 1), 0)) & (seq_len - 1)
    has_left = pos != 0
    has_right = pos != seq_len - 1

    for c in range(d // tc):
        cols = slice(c * tc, (c + 1) * tc)
        b_gate = _dot(h_ref[HALO:HALO + tm, :], win_ref[:, c * tc:(c + 1) * tc])
        c_gate = _dot(h_ref[...], win_ref[:, d + c * tc:d + (c + 1) * tc])
        x_in = _dot(h_ref[...], win_ref[:, 2 * d + c * tc:2 * d + (c + 1) * tc])
        u_ref[...] = c_gate * x_in
        cw = cw_ref[:, cols]
        y = (cw[0:1, :] * jnp.where(has_left, u_ref[HALO - 1:HALO - 1 + tm, :], 0.0)
             + cw[1:2, :] * u_ref[HALO:HALO + tm, :]
             + cw[2:3, :] * jnp.where(has_right, u_ref[HALO + 1:HALO + 1 + tm, :], 0.0))
        z_ref[:, cols] = (b_gate * y).astype(BF16)
    out = _dot(z_ref[...], wout_ref[...])
    o_ref[...] = x_ref[...] + gate * _rms(out, gpost_ref[...])


def _conv_mixer(x, mod_l, g_pre, g_post, w_in, conv_w, w_out, rows):
    m, d = x.shape
    tm = rows.tile
    tc = 256
    assert rows.ctx_len & (rows.ctx_len - 1) == 0 and rows.lat_len & (rows.lat_len - 1) == 0
    hb = tm // HALO
    last = m // HALO - 1
    return pl.pallas_call(
        functools.partial(_conv_kernel, tc=tc, rows=rows),
        out_shape=jax.ShapeDtypeStruct((m, d), F32),
        grid=(rows.n_tiles,),
        in_specs=[pl.BlockSpec((tm, d), lambda i: (i, 0)),
                  pl.BlockSpec((HALO, d), lambda i: (jnp.maximum(i * hb - 1, 0), 0)),
                  pl.BlockSpec((HALO, d), lambda i: (jnp.minimum((i + 1) * hb, last), 0)),
                  rows.mod_spec(mod_l.shape[-1]),
                  pl.BlockSpec((1, d), lambda i: (0, 0)),
                  pl.BlockSpec((1, d), lambda i: (0, 0)),
                  pl.BlockSpec((d, 3 * d), lambda i: (0, 0), pipeline_mode=pl.Buffered(1)),
                  pl.BlockSpec((CONV_W, d), lambda i: (0, 0)),
                  pl.BlockSpec((d, d), lambda i: (0, 0), pipeline_mode=pl.Buffered(1))],
        out_specs=pl.BlockSpec((tm, d), lambda i: (i, 0)),
        scratch_shapes=[pltpu.VMEM((tm + 2 * HALO, d), BF16),
                        pltpu.VMEM((tm + 2 * HALO, tc), F32),
                        pltpu.VMEM((tm, d), BF16)],
        compiler_params=_params("parallel"),
        name="conv_mixer",
    )(x, x, x, mod_l, g_pre, g_post, w_in, conv_w, w_out)


def _pair_codes(c, forward):
    t = lax.broadcasted_iota(jnp.int32, (c, c), 0)
    s = lax.broadcasted_iota(jnp.int32, (c, c), 1)
    x = t ^ s
    n_levels = int(math.log2(c // SCAN_GROUP))
    code = jnp.full((c, c), n_levels, jnp.int32)
    for lvl in range(n_levels - 1, 0, -1):
        code = jnp.where(x >= (c >> lvl), lvl, code)
    code = jnp.where(x < SCAN_GROUP, n_levels + 1, code)
    visited = (s <= t) if forward else (s >= t)
    return jnp.where(visited, code, 0), n_levels


def _cumsum_matrix(c_len, forward):
    t = lax.broadcasted_iota(jnp.int32, (c_len, c_len), 0)
    s = lax.broadcasted_iota(jnp.int32, (c_len, c_len), 1)
    return jnp.where((s <= t) if forward else (s >= t), 1.0, 0.0).astype(BF16)


def _gate_chunk(z, lb, cum, k_ref, c_ref, base):
    c_len, dk = z.shape
    e_abs = jnp.exp(-jnp.abs(z))
    inv = 1.0 / (1.0 + e_abs)
    sig = jnp.where(z >= 0, inv, e_abs * inv)
    sig_neg = jnp.where(z >= 0, e_abs * inv, inv)
    f = lb + (1.0 - lb) * sig
    log_sig = jnp.minimum(z, 0.0) + jnp.log(inv)
    log_f = jnp.maximum(jnp.log1p(-lb) + log_sig, jnp.log(jnp.maximum(f, 1e-30)))
    k_ref[pl.ds(base, c_len), :] = (1.0 - lb) * sig_neg

    hi = log_f.astype(BF16)
    rest = log_f - hi.astype(F32)
    mid = rest.astype(BF16)
    lo = (rest - mid.astype(F32)).astype(BF16)
    sums = _dot(cum, jnp.concatenate([hi, mid, lo], axis=1))
    c = sums[:, :dk] + sums[:, dk:2 * dk] + sums[:, 2 * dk:]
    c_ref[pl.ds(base, c_len), :] = c

    span = jnp.zeros((1, dk), F32)
    for g in range(0, c_len, SCAN_GROUP):
        ends = c_ref[pl.ds(base + g, 1), :] - c_ref[pl.ds(base + g + SCAN_GROUP - 1, 1), :]
        span = jnp.maximum(span, jnp.abs(ends))
    return span


def _scan_chunk(q, k, c, v, c_row, st_ref, code, n_levels, forward, factorised):
    c_len, dk = q.shape

    def reference_rows(block, offset):
        parts = [jnp.broadcast_to(c_row(n * block + offset), (block, dk)) for n in range(c_len // block)]
        return parts[0] if len(parts) == 1 else jnp.concatenate(parts, axis=0)

    att = jnp.zeros((c_len, c_len), F32)
    for lvl in range(1, n_levels + 1):
        half = c_len >> lvl
        e = jnp.exp(-jnp.abs(c - reference_rows(2 * half, half)))
        att = jnp.where(code == lvl, _dot_nt((q * e).astype(BF16), (k * e).astype(BF16)), att)

    if factorised:
        dq = c - reference_rows(SCAN_GROUP, 0 if forward else SCAN_GROUP - 1)
        same = _dot_nt((q * jnp.exp(dq)).astype(BF16), (k * jnp.exp(-dq)).astype(BF16))
    else:
        row = lax.broadcasted_iota(jnp.int32, (c_len, dk), 0)
        t = lax.broadcasted_iota(jnp.int32, (c_len, c_len), 0)
        s = lax.broadcasted_iota(jnp.int32, (c_len, c_len), 1)
        pos = row & (SCAN_GROUP - 1)
        same = jnp.zeros((c_len, c_len), F32)
        for dist in range(SCAN_GROUP):
            if dist == 0:
                k_s, c_s = k, c
            else:
                shift = dist if forward else c_len - dist
                k_s, c_s = pltpu.roll(k, shift, 0), pltpu.roll(c, shift, 0)
            valid = (pos >= dist) if forward else (pos + dist < SCAN_GROUP)
            term = q * k_s * jnp.exp(jnp.where(valid, c - c_s, 0.0))
            a = jnp.sum(jnp.where(valid, term, 0.0), axis=-1, keepdims=True)
            same = jnp.where(s == (t - dist if forward else t + dist), a, same)
    att = jnp.where(code == n_levels + 1, same, att)

    st = st_ref[...]
    o = _dot(att.astype(BF16), v.astype(BF16)) + _dot_nt((q * jnp.exp(c)).astype(BF16), st.astype(BF16))
    c_end = c_row(c_len - 1 if forward else 0)
    k_dec = (k * jnp.exp(c_end - c)).astype(BF16)
    st_ref[...] = st * jnp.exp(c_end) + _dot_tn(v.astype(BF16), k_dec)
    return o


def _hgrn_kernel(*refs, layer_j, chunk, has_s0):
    q_ref, zf_ref, zb_ref, v_ref, g_ref, alb_ref, gn_ref = refs[:7]
    o_ref, s_ref, of_ref, ob_ref, k_ref, c_ref = refs[-6:]
    seq = q_ref.shape[0]
    n_chunks = seq // chunk

    def chunk_base(n):
        return pl.multiple_of(n * chunk, chunk)

    a = alb_ref[...]
    e = jnp.exp(a - jnp.max(a, axis=0, keepdims=True))
    p = e / jnp.sum(e, axis=0, keepdims=True)
    cum = p[0]
    for n in range(1, layer_j + 1):
        cum = cum + p[n]
    lb = jnp.clip(cum - p[0], 0.0, LB_MAX)

    cum_f = _cumsum_matrix(chunk, True)
    cum_b = _cumsum_matrix(chunk, False)

    def gates(n, span):
        base = chunk_base(n)
        rows = pl.ds(base, chunk)
        span_f = _gate_chunk(zf_ref[rows, :], lb[0:1], cum_f, k_ref.at[0], c_ref.at[0], base)
        span_b = _gate_chunk(zb_ref[rows, :], lb[1:2], cum_b, k_ref.at[1], c_ref.at[1], base)
        return jnp.maximum(span, jnp.maximum(span_f, span_b))

    unroll = 2 if n_chunks % 2 == 0 else 1
    span = lax.fori_loop(0, n_chunks, gates, jnp.zeros((1, q_ref.shape[1]), F32), unroll=unroll)
    safe = jnp.max(span) <= SAFE_EXP

    s_ref[...] = refs[7][...] if has_s0 else jnp.zeros(s_ref.shape, F32)
    code_f, n_levels = _pair_codes(chunk, True)
    code_b, _ = _pair_codes(chunk, False)

    def scan(factorised):
        def one(direction, base, code):
            rows = pl.ds(base, chunk)
            c_dir = c_ref.at[direction]
            return _scan_chunk(q_ref[rows, :], k_ref[direction, rows, :], c_dir[rows, :], v_ref[rows, :],
                               lambda r: c_dir[pl.ds(base + r, 1), :], s_ref.at[direction], code, n_levels,
                               direction == 0, factorised)

        def body(n, carry):
            base_f = chunk_base(n)
            base_b = chunk_base(n_chunks - 1 - n)
            of_ref[pl.ds(base_f, chunk), :] = one(0, base_f, code_f)
            ob_ref[pl.ds(base_b, chunk), :] = one(1, base_b, code_b)
            return carry

        lax.fori_loop(0, n_chunks, body, 0, unroll=unroll if factorised else 1)

    lax.cond(safe, lambda: scan(True), lambda: scan(False))

    o = of_ref[...] + ob_ref[...]
    o_ref[...] = _rms(o, gn_ref[...]) * _silu(g_ref[...])


def _hgrn(proj, row_block0, seq, n_batch, a_lb, gnorm, s0_t, layer_j):
    dk = LANES
    heads = A_HEADS
    chunk = min(SCAN_CHUNK, seq)
    state_spec = pl.BlockSpec((None, 2, None, dk, dk), lambda b, h: (b, 0, h, 0, 0))

    def col(section):
        return pl.BlockSpec((seq, dk), lambda b, h: (row_block0 + b, section * heads + h))

    return pl.pallas_call(
        functools.partial(_hgrn_kernel, layer_j=layer_j, chunk=chunk, has_s0=s0_t is not None),
        out_shape=(jax.ShapeDtypeStruct((n_batch * seq, heads * dk), F32),
                   jax.ShapeDtypeStruct((n_batch, 2, heads, dk, dk), F32)),
        grid=(n_batch, heads),
        in_specs=[col(0), col(1), col(2), col(3), col(4),
                  pl.BlockSpec((a_lb.shape[0], 2, dk), lambda b, h: (0, 0, h)),
                  pl.BlockSpec((1, dk), lambda b, h: (0, 0))] + ([] if s0_t is None else [state_spec]),
        out_specs=(pl.BlockSpec((seq, dk), lambda b, h: (b, h)), state_spec),
        scratch_shapes=[pltpu.VMEM((seq, dk), F32), pltpu.VMEM((seq, dk), F32),
                        pltpu.VMEM((2, seq, dk), F32), pltpu.VMEM((2, seq, dk), F32)],
        compiler_params=_params("parallel", "parallel"),
        name="hgrn2_scan",
    )(proj, proj, proj, proj, proj, a_lb, gnorm, *(() if s0_t is None else (s0_t,)))


def _head_lanes(shape, head):
    lane = lax.broadcasted_iota(jnp.int32, shape, len(shape) - 1)
    half = shape[-1] // 2
    return (lane < half) if head == 0 else (lane >= half)


def _ctx_attn_kernel(q_ref, k_ref, v_ref, o_ref, *, scale):
    for pair in range(q_ref.shape[1] // LANES):
        lanes = slice(pair * LANES, (pair + 1) * LANES)
        k = k_ref[:, lanes].astype(BF16)
        v = v_ref[:, lanes].astype(BF16)
        q = q_ref[:, lanes]
        outs = []
        for head in range(2):
            qh = jnp.where(_head_lanes(q.shape, head), q, 0.0).astype(BF16)
            s = _dot_nt(qh, k) * scale
            p = jnp.exp(s - jnp.max(s, axis=-1, keepdims=True))
            outs.append(_dot(p.astype(BF16), v) / jnp.sum(p, axis=-1, keepdims=True))
        o_ref[:, lanes] = jnp.where(_head_lanes(q.shape, 0), outs[0], outs[1])


def _ctx_attention(proj, n_batch, seq, col0):
    width = B_HEADS // 2 * LANES
    cb = col0 // width
    assert col0 % width == 0

    def col(section):
        return pl.BlockSpec((seq, width), lambda b: (b, cb + section))

    return pl.pallas_call(
        functools.partial(_ctx_attn_kernel, scale=1.0 / math.sqrt(LANES // 2)),
        out_shape=jax.ShapeDtypeStruct((n_batch * seq, width), F32),
        grid=(n_batch,),
        in_specs=[col(0), col(1), col(2)],
        out_specs=pl.BlockSpec((seq, width), lambda b: (b, 0)),
        compiler_params=_params("parallel"),
        name="context_attention",
    )(proj, proj, proj)


def _rpb_tiles_kernel(rpb_ref, o_ref):
    h = pl.program_id(0)
    qc = lax.broadcasted_iota(jnp.int32, (GRID_W, GRID_W), 0)
    kc = lax.broadcasted_iota(jnp.int32, (GRID_W, GRID_W), 1)
    c0 = jnp.clip(qc - NA_COLS // 2, 0, GRID_W - NA_COLS)
    col_ok = (kc >= c0) & (kc < c0 + NA_COLS)
    dc = jnp.clip(kc - qc, -(NA_COLS - 1), NA_COLS - 1) + NA_COLS - 1
    for dr in range(2 * NA_ROWS - 1):
        t = jnp.zeros((GRID_W, GRID_W), F32)
        for i in range(2 * NA_COLS - 1):
            t = jnp.where(dc == i, rpb_ref[h, dr, i], t)
        o_ref[dr] = jnp.where(col_ok, t, MASKED)


def _rpb_tiles(rpb):
    heads, n_dr, n_dc = rpb.shape
    return pl.pallas_call(
        _rpb_tiles_kernel,
        out_shape=jax.ShapeDtypeStruct((heads, n_dr, GRID_W, GRID_W), F32),
        grid=(heads,),
        in_specs=[pl.BlockSpec(memory_space=pltpu.SMEM)],
        out_specs=pl.BlockSpec((None, n_dr, GRID_W, GRID_W), lambda h: (h, 0, 0, 0)),
        compiler_params=_params("parallel"),
        name="rpb_tiles",
    )(rpb)


def _nbr_geometry(grid_rows):
    win = min(NA_ROWS, grid_rows)
    blocks = []
    for qb in range(grid_rows // Q_ROWS):
        r0 = qb * Q_ROWS
        key0 = min(max(r0 - win // 2, 0), grid_rows - KEY_ROWS)
        tiles = []
        for i in range(Q_ROWS):
            r = r0 + i
            start = min(max(r - win // 2, 0), grid_rows - win)
            assert key0 <= start and start + win <= key0 + KEY_ROWS
            tiles.append([(key0 + jr - r + NA_ROWS - 1) if start <= key0 + jr < start + win else None
                          for jr in range(KEY_ROWS)])
        blocks.append((key0, tiles))
    return blocks


def _nbr_attn_kernel(q_ref, k_ref, v_ref, ck_ref, cv_ref, tiles_ref, o_ref, bias_ref, *, scale, geometry):
    w = GRID_W
    nq = Q_ROWS * w
    nk = KEY_ROWS * w

    @pl.when(pl.program_id(1) == 0)
    def _():
        masked = jnp.full((w, w), MASKED, F32)
        for head in range(2):
            for qb, (_, tiles) in enumerate(geometry):
                for i, row_tiles in enumerate(tiles):
                    for jr, dr in enumerate(row_tiles):
                        tile = masked if dr is None else tiles_ref[head, dr]
                        bias_ref[head, qb, i * w:(i + 1) * w, jr * w:(jr + 1) * w] = tile

    ck = ck_ref[...].astype(BF16)
    cv = cv_ref[...].astype(BF16)
    for qb, (key0, _) in enumerate(geometry):
        q = q_ref[qb * nq:(qb + 1) * nq, :]
        k = k_ref[key0 * w:key0 * w + nk, :].astype(BF16)
        v = v_ref[key0 * w:key0 * w + nk, :].astype(BF16)
        outs = []
        for head in range(2):
            qh = jnp.where(_head_lanes(q.shape, head), q, 0.0).astype(BF16)
            s_loc = _dot_nt(qh, k) * scale + bias_ref[head, qb]
            s_ctx = _dot_nt(qh, ck) * scale
            m = jnp.maximum(jnp.max(s_loc, axis=-1, keepdims=True), jnp.max(s_ctx, axis=-1, keepdims=True))
            p_loc = jnp.exp(s_loc - m)
            p_ctx = jnp.exp(s_ctx - m)
            denom = jnp.sum(p_loc, axis=-1, keepdims=True) + jnp.sum(p_ctx, axis=-1, keepdims=True)
            outs.append((_dot(p_loc.astype(BF16), v) + _dot(p_ctx.astype(BF16), cv)) / denom)
        o_ref[qb * nq:(qb + 1) * nq, :] = jnp.where(_head_lanes(q.shape, 0), outs[0], outs[1])


def _nbr_attention(proj, row_block0, seq, n_batch, col0, ck, cv, tiles):
    pairs = B_HEADS // 2
    cb = col0 // LANES
    past = ck.shape[2]
    grid_rows = seq // GRID_W
    assert grid_rows % Q_ROWS == 0 and grid_rows >= KEY_ROWS
    geometry = _nbr_geometry(grid_rows)

    def col(section):
        return pl.BlockSpec((seq, LANES), lambda h, b: (row_block0 + b, cb + section * pairs + h))

    return pl.pallas_call(
        functools.partial(_nbr_attn_kernel, scale=1.0 / math.sqrt(LANES // 2), geometry=geometry),
        out_shape=jax.ShapeDtypeStruct((n_batch * seq, pairs * LANES), F32),
        grid=(pairs, n_batch),
        in_specs=[col(0), col(1), col(2),
                  pl.BlockSpec((None, None, past, LANES), lambda h, b: (b, h, 0, 0)),
                  pl.BlockSpec((None, None, past, LANES), lambda h, b: (b, h, 0, 0)),
                  pl.BlockSpec((2,) + tiles.shape[1:], lambda h, b: (h, 0, 0, 0))],
        out_specs=pl.BlockSpec((seq, LANES), lambda h, b: (b, h)),
        scratch_shapes=[pltpu.VMEM((2, len(geometry), Q_ROWS * GRID_W, KEY_ROWS * GRID_W), F32)],
        compiler_params=_params("arbitrary", "arbitrary"),
        name="neighbourhood_attention",
    )(proj, proj, proj, ck, cv, tiles)


def _head_pairs(t):
    b, h, l, dh = t.shape
    return t.reshape(b, h // 2, 2, l, dh).transpose(0, 1, 3, 2, 4).reshape(b, h // 2, l, 2 * dh)


def _heads_major(t, n_batch, seq):
    return t.reshape(n_batch, seq, B_HEADS, -1).transpose(0, 2, 1, 3)


def kernel(x_prompt, x_sample, c, state_hgrn, cache_k, cache_v, c_ctx, mod_w, mod_b, norm_pre, norm_post,
           ffn_w_in, ffn_w_out, ab_w_in, ab_w_out, a_lb, a_gnorm, b_rpb, cv_w_in, cv_conv, cv_w_out):
    n_ctx, ctx_len, d = x_prompt.shape
    n_lat, lat_len, _ = x_sample.shape
    depth = mod_w.shape[0]
    d_a = a_lb.shape[-1]
    rows = _Rows(n_ctx * ctx_len, ctx_len, n_lat * lat_len, lat_len, ROW_TILE)
    assert (n_ctx * ctx_len) % lat_len == 0
    lat_block0 = (n_ctx * ctx_len) // lat_len

    x = (x_prompt.reshape(-1, d), x_sample.reshape(-1, d))

    cond = jnp.concatenate([c_ctx[None, :], c], axis=0)
    cond = jnp.pad(cond, ((0, -cond.shape[0] % 8), (0, 0)))
    mod = _modulation(cond, mod_w, mod_b)
    mod = mod.reshape(depth, cond.shape[0], 1, N_MOD * d)

    ffn_w_in = ffn_w_in.astype(BF16)
    ffn_w_out = ffn_w_out.astype(BF16)
    ab_w_in = ab_w_in.astype(BF16)
    ab_w_out = ab_w_out.astype(BF16)
    cv_w_in = cv_w_in.astype(BF16)
    cv_w_out = cv_w_out.astype(BF16)

    new_s, new_k, new_v = [], [], []
    for l in range(depth):
        j = l // 2

        def gains(t, sub):
            return t[l, sub][None, :]

        x = _ffn(x, mod[l], gains(norm_pre, 0), gains(norm_post, 0), ffn_w_in[l, 0], ffn_w_out[l, 0], rows, 0)
        if l % 2 == 0:
            proj = _inproj(x, mod[l], gains(norm_pre, 1), ab_w_in[j], rows)
            gn = a_gnorm[j][None, :]
            oa_ctx, s_ctx = _hgrn(proj, 0, ctx_len, n_ctx, a_lb, gn, None, j)
            s_lat = jnp.swapaxes(state_hgrn[:, j].astype(F32), -1, -2)
            oa_lat, _ = _hgrn(proj, lat_block0, lat_len, n_lat, a_lb, gn, s_lat, j)
            ob_ctx = _ctx_attention(proj, n_ctx, ctx_len, 5 * d_a)
            ob_lat = _nbr_attention(proj, lat_block0, lat_len, n_lat, 5 * d_a,
                                    _head_pairs(cache_k[:, j]), _head_pairs(cache_v[:, j]),
                                    _rpb_tiles(b_rpb[j]))
            d_b = ob_ctx.shape[1]
            k_cols = proj[:n_ctx * ctx_len, 5 * d_a + d_b:5 * d_a + 2 * d_b]
            v_cols = proj[:n_ctx * ctx_len, 5 * d_a + 2 * d_b:5 * d_a + 3 * d_b]
            new_s.append(jnp.swapaxes(s_ctx, -1, -2))
            new_k.append(_heads_major(k_cols, n_ctx, ctx_len))
            new_v.append(_heads_major(v_cols, n_ctx, ctx_len))
            x = _outproj(x, (oa_ctx, oa_lat), (ob_ctx, ob_lat), mod[l], gains(norm_post, 1), ab_w_out[j], rows)
        else:
            x = _conv_mixer(x, mod[l], gains(norm_pre, 1), gains(norm_post, 1), cv_w_in[j], cv_conv[j],
                            cv_w_out[j], rows)
        x = _ffn(x, mod[l], gains(norm_pre, 2), gains(norm_post, 2), ffn_w_in[l, 1], ffn_w_out[l, 1], rows, 2,
                 split_out=(l == depth - 1))

    return (x[0].reshape(x_prompt.shape), x[1].reshape(x_sample.shape),
            jnp.stack(new_s, axis=1), jnp.stack(new_k, axis=1), jnp.stack(new_v, axis=1))
```

```python
import functools
import math

import jax
import jax.numpy as jnp
import numpy as np
from jax import lax
from jax.experimental import pallas as pl
from jax.experimental.pallas import tpu as pltpu

F32 = jnp.float32
BF16 = jnp.bfloat16

EPS = 1e-6
LB_MAX = 1.0 - 1e-4
MASKED = -1e30

GRID_W = 64
A_HEADS = 4
B_HEADS = 8
NA_ROWS = 8
NA_COLS = 16
CONV_W = 3
N_MOD = 9

VMEM_LIMIT_BYTES = 56 * 1024 * 1024
LANES = 128

ROW_TILE = 512
FFN_SUB_TILES = 2
HALO = 16
SCAN_CHUNK = 128
SCAN_GROUP = 16
SAFE_EXP = 80.0
Q_ROWS = 8
KEY_ROWS = 16


def _params(*sem):
    return pltpu.CompilerParams(dimension_semantics=sem, vmem_limit_bytes=VMEM_LIMIT_BYTES)


def _rms(x, g):
    return x * lax.rsqrt(jnp.mean(x * x, axis=-1, keepdims=True) + EPS) * g


def _silu(x):
    return x * jax.nn.sigmoid(x)


def _dot(a, b):
    return jnp.dot(a, b, preferred_element_type=F32)


def _dot_nt(a, b):
    return lax.dot_general(a, b, (((1,), (1,)), ((), ())), preferred_element_type=F32)


def _dot_tn(a, b):
    return lax.dot_general(a, b, (((0,), (0,)), ((), ())), preferred_element_type=F32)


def _resident_spec(w, index):
    lead = len(index)
    zeros = (0,) * (w.ndim - lead)
    return pl.BlockSpec((None,) * lead + tuple(w.shape[lead:]), lambda *_: tuple(index) + zeros,
                        pipeline_mode=pl.Buffered(1))


def _mod_parts(mod, j, d):
    return (mod[:, (3 * j) * d:(3 * j + 1) * d],
            mod[:, (3 * j + 1) * d:(3 * j + 2) * d],
            mod[:, (3 * j + 2) * d:(3 * j + 3) * d])


class _Rows:
    def __init__(self, n_ctx_rows, ctx_len, n_lat_rows, lat_len, tile):
        assert ctx_len % tile == 0 or tile % ctx_len == 0
        assert lat_len % tile == 0 and n_ctx_rows % tile == 0 and n_lat_rows % tile == 0
        self.tile = tile
        self.n_ctx_tiles = n_ctx_rows // tile
        self.n_tiles = (n_ctx_rows + n_lat_rows) // tile
        self.lat_tiles_per_batch = lat_len // tile
        self.ctx_len = ctx_len
        self.lat_len = lat_len

    def mod_row(self, i):
        return jnp.where(i < self.n_ctx_tiles, 0, 1 + (i - self.n_ctx_tiles) // self.lat_tiles_per_batch)

    def group_specs(self, width):
        nc = self.n_ctx_tiles
        return [pl.BlockSpec((self.tile, width), lambda i: (jnp.minimum(i, nc - 1), 0)),
                pl.BlockSpec((self.tile, width), lambda i: (jnp.maximum(i - nc, 0), 0))]

    def mod_spec(self, width):
        return pl.BlockSpec((None, 1, width), lambda i: (self.mod_row(i), 0, 0))


def _mod_kernel(cond_ref, w_ref, b_ref, o_ref):
    s = _silu(cond_ref[...]).astype(BF16)
    o_ref[...] = _dot(s, w_ref[...].astype(BF16)) + b_ref[...]


def _modulation(cond, mod_w, mod_b):
    depth, d, n = mod_w.shape
    rows = cond.shape[0]
    tn = 1536
    assert n % tn == 0
    return pl.pallas_call(
        _mod_kernel,
        out_shape=jax.ShapeDtypeStruct((depth, rows, n), F32),
        grid=(depth, n // tn),
        in_specs=[pl.BlockSpec((rows, d), lambda l, j: (0, 0)),
                  pl.BlockSpec((None, d, tn), lambda l, j: (l, 0, j)),
                  pl.BlockSpec((None, 1, tn), lambda l, j: (l, 0, j))],
        out_specs=pl.BlockSpec((None, rows, tn), lambda l, j: (l, 0, j)),
        compiler_params=_params("parallel", "parallel"),
        name="adaln_mod",
    )(cond, mod_w, mod_b.reshape(depth, 1, n))


def _ffn_kernel(*refs, j, tf, n_sub, n_ctx_tiles, split_in, split_out):
    n_x = 2 if split_in else 1
    x_refs = refs[:n_x]
    mod_ref, gpre_ref, gpost_ref, win_ref, wout_ref = refs[n_x:n_x + 5]
    o_refs = refs[n_x + 5:-2]
    h_ref, act_ref = refs[-2:]
    is_ctx = pl.program_id(0) < n_ctx_tiles

    def load_x(rows):
        if split_in:
            return jnp.where(is_ctx, x_refs[0][rows, :], x_refs[1][rows, :])
        return x_refs[0][rows, :]

    tm, d = x_refs[0].shape
    dff = wout_ref.shape[0]
    shift, scale, gate = _mod_parts(mod_ref[...], j, d)
    for sub in range(n_sub):
        rows = slice(sub * (tm // n_sub), (sub + 1) * (tm // n_sub))
        h_ref[rows, :] = (_rms(load_x(rows), gpre_ref[...]) * (1.0 + scale) + shift).astype(BF16)
        for c in range(dff // tf):
            g = _dot(h_ref[rows, :], win_ref[:, c * tf:(c + 1) * tf])
            u = _dot(h_ref[rows, :], win_ref[:, dff + c * tf:dff + (c + 1) * tf])
            act_ref[rows, c * tf:(c + 1) * tf] = (_silu(g) * u).astype(BF16)
        out = _dot(act_ref[rows, :], wout_ref[...])
        y = load_x(rows) + (0.5 * gate) * _rms(out, gpost_ref[...])
        if split_out:
            @pl.when(is_ctx)
            def _():
                o_refs[0][rows, :] = y

            @pl.when(jnp.logical_not(is_ctx))
            def _():
                o_refs[1][rows, :] = y
        else:
            o_refs[0][rows, :] = y


def _ffn(x, mod_l, g_pre, g_post, w_in, w_out, rows, j, split_out=False):
    split_in = isinstance(x, tuple)
    d = x[0].shape[1] if split_in else x.shape[1]
    dff = w_out[0].shape[-2]
    tm = rows.tile
    tf = 256
    assert dff % tf == 0
    out_shape = jax.ShapeDtypeStruct((rows.n_tiles * tm, d), F32)
    out_spec = pl.BlockSpec((tm, d), lambda i: (i, 0))
    if split_out:
        out_shape = (jax.ShapeDtypeStruct((rows.n_ctx_tiles * tm, d), F32),
                     jax.ShapeDtypeStruct(((rows.n_tiles - rows.n_ctx_tiles) * tm, d), F32))
        out_spec = tuple(rows.group_specs(d))
    return pl.pallas_call(
        functools.partial(_ffn_kernel, j=j, tf=tf, n_sub=FFN_SUB_TILES, n_ctx_tiles=rows.n_ctx_tiles,
                          split_in=split_in, split_out=split_out),
        out_shape=out_shape,
        grid=(rows.n_tiles,),
        in_specs=(rows.group_specs(d) if split_in else [pl.BlockSpec((tm, d), lambda i: (i, 0))]) + [
                  rows.mod_spec(mod_l.shape[-1]),
                  pl.BlockSpec((1, d), lambda i: (0, 0)),
                  pl.BlockSpec((1, d), lambda i: (0, 0)),
                  _resident_spec(*w_in), _resident_spec(*w_out)],
        out_specs=out_spec,
        scratch_shapes=[pltpu.VMEM((tm, d), BF16), pltpu.VMEM((tm, dff), BF16)],
        compiler_params=_params("arbitrary" if split_out else "parallel"),
        name="swiglu_sublayer",
    )(*(x if split_in else (x,)), mod_l, g_pre, g_post, w_in[0], w_out[0])


def _inproj_kernel(x_ref, mod_ref, gpre_ref, w_ref, o_ref, h_ref, *, tn):
    d = x_ref.shape[1]
    shift, scale, _ = _mod_parts(mod_ref[...], 1, d)
    h_ref[...] = (_rms(x_ref[...], gpre_ref[...]) * (1.0 + scale) + shift).astype(BF16)
    for c in range(w_ref.shape[1] // tn):
        o_ref[:, c * tn:(c + 1) * tn] = _dot(h_ref[...], w_ref[:, c * tn:(c + 1) * tn])


def _inproj(x, mod_l, g_pre, w, rows):
    m, d = x.shape
    n = w[0].shape[-1]
    tm = rows.tile
    tn = 512
    assert n % tn == 0
    return pl.pallas_call(
        functools.partial(_inproj_kernel, tn=tn),
        out_shape=jax.ShapeDtypeStruct((m, n), F32),
        grid=(rows.n_tiles,),
        in_specs=[pl.BlockSpec((tm, d), lambda i: (i, 0)),
                  rows.mod_spec(mod_l.shape[-1]),
                  pl.BlockSpec((1, d), lambda i: (0, 0)),
                  _resident_spec(*w)],
        out_specs=pl.BlockSpec((tm, n), lambda i: (i, 0)),
        scratch_shapes=[pltpu.VMEM((tm, d), BF16)],
        compiler_params=_params("parallel"),
        name="mixer_in_proj",
    )(x, mod_l, g_pre, w[0])


def _outproj_kernel(x_ref, ac_ref, al_ref, bc_ref, bl_ref, mod_ref, gpost_ref, w_ref, o_ref, *, n_ctx_tiles):
    d = x_ref.shape[1]
    da = ac_ref.shape[1]
    is_ctx = pl.program_id(0) < n_ctx_tiles
    _, _, gate = _mod_parts(mod_ref[...], 1, d)
    a = jnp.where(is_ctx, ac_ref[...], al_ref[...]).astype(BF16)
    b = jnp.where(is_ctx, bc_ref[...], bl_ref[...]).astype(BF16)
    out = _dot(a, w_ref[:da, :]) + _dot(b, w_ref[da:, :])
    o_ref[...] = x_ref[...] + gate * _rms(out, gpost_ref[...])


def _outproj(x, o_a, o_b, mod_l, g_post, w, rows):
    m, d = x.shape
    tm = rows.tile
    return pl.pallas_call(
        functools.partial(_outproj_kernel, n_ctx_tiles=rows.n_ctx_tiles),
        out_shape=jax.ShapeDtypeStruct((m, d), F32),
        grid=(rows.n_tiles,),
        in_specs=[pl.BlockSpec((tm, d), lambda i: (i, 0))]
                 + rows.group_specs(o_a[0].shape[1]) + rows.group_specs(o_b[0].shape[1]) + [
                  rows.mod_spec(mod_l.shape[-1]),
                  pl.BlockSpec((1, d), lambda i: (0, 0)),
                  _resident_spec(*w)],
        out_specs=pl.BlockSpec((tm, d), lambda i: (i, 0)),
        compiler_params=_params("parallel"),
        name="mixer_out_proj",
    )(x, *o_a, *o_b, mod_l, g_post, w[0])


def _conv_kernel(x_ref, xp_ref, xn_ref, mod_ref, gpre_ref, gpost_ref, win_ref, cw_ref, wout_ref, o_ref,
                 h_ref, u_ref, z_ref, *, tc, rows):
    tm, d = x_ref.shape
    i = pl.program_id(0)
    shift, scale, gate = _mod_parts(mod_ref[...], 1, d)
    g_pre = gpre_ref[...]

    def pre(x):
        return (_rms(x, g_pre) * (1.0 + scale) + shift).astype(BF16)

    h_ref[0:HALO, :] = pre(xp_ref[...])
    h_ref[HALO:HALO + tm, :] = pre(x_ref[...])
    h_ref[HALO + tm:, :] = pre(xn_ref[...])

    seq_len = jnp.where(i < rows.n_ctx_tiles, rows.ctx_len, rows.lat_len)
    pos = (i * tm + lax.broadcasted_iota(jnp.int32, (tm, 1), 0)) & (seq_len - 1)
    has_left = pos != 0
    has_right = pos != seq_len - 1

    for c in range(d // tc):
        cols = slice(c * tc, (c + 1) * tc)
        b_gate = _dot(h_ref[HALO:HALO + tm, :], win_ref[:, c * tc:(c + 1) * tc])
        c_gate = _dot(h_ref[...], win_ref[:, d + c * tc:d + (c + 1) * tc])
        x_in = _dot(h_ref[...], win_ref[:, 2 * d + c * tc:2 * d + (c + 1) * tc])
        u_ref[...] = c_gate * x_in
        cw = cw_ref[:, cols]
        y = (cw[0:1, :] * jnp.where(has_left, u_ref[HALO - 1:HALO - 1 + tm, :], 0.0)
             + cw[1:2, :] * u_ref[HALO:HALO + tm, :]
             + cw[2:3, :] * jnp.where(has_right, u_ref[HALO + 1:HALO + 1 + tm, :], 0.0))
        z_ref[:, cols] = (b_gate * y).astype(BF16)
    out = _dot(z_ref[...], wout_ref[...])
    o_ref[...] = x_ref[...] + gate * _rms(out, gpost_ref[...])


def _conv_mixer(x, mod_l, g_pre, g_post, w_in, conv_w, w_out, rows):
    m, d = x.shape
    tm = rows.tile
    tc = 256
    assert rows.ctx_len & (rows.ctx_len - 1) == 0 and rows.lat_len & (rows.lat_len - 1) == 0
    hb = tm // HALO
    last = m // HALO - 1
    return pl.pallas_call(
        functools.partial(_conv_kernel, tc=tc, rows=rows),
        out_shape=jax.ShapeDtypeStruct((m, d), F32),
        grid=(rows.n_tiles,),
        in_specs=[pl.BlockSpec((tm, d), lambda i: (i, 0)),
                  pl.BlockSpec((HALO, d), lambda i: (jnp.maximum(i * hb - 1, 0), 0)),
                  pl.BlockSpec((HALO, d), lambda i: (jnp.minimum((i + 1) * hb, last), 0)),
                  rows.mod_spec(mod_l.shape[-1]),
                  pl.BlockSpec((1, d), lambda i: (0, 0)),
                  pl.BlockSpec((1, d), lambda i: (0, 0)),
                  _resident_spec(*w_in),
                  pl.BlockSpec((CONV_W, d), lambda i: (0, 0)),
                  _resident_spec(*w_out)],
        out_specs=pl.BlockSpec((tm, d), lambda i: (i, 0)),
        scratch_shapes=[pltpu.VMEM((tm + 2 * HALO, d), BF16),
                        pltpu.VMEM((tm + 2 * HALO, tc), F32),
                        pltpu.VMEM((tm, d), BF16)],
        compiler_params=_params("parallel"),
        name="conv_mixer",
    )(x, x, x, mod_l, g_pre, g_post, w_in[0], conv_w, w_out[0])


def _scan_constants(c):
    t, s = np.meshgrid(np.arange(c), np.arange(c), indexing="ij")
    x = t ^ s
    n_levels = int(math.log2(c // SCAN_GROUP))
    code = np.full((c, c), n_levels, np.int32)
    for lvl in range(n_levels - 1, 0, -1):
        code = np.where(x >= (c >> lvl), lvl, code)
    code = np.where(x < SCAN_GROUP, n_levels + 1, code)
    visited = np.stack([s <= t, s >= t])
    codes = np.where(visited, code[None], 0).astype(np.int32)
    return jnp.asarray(codes), jnp.asarray(visited, BF16), n_levels


def _gate_chunk(z, lb, cum, k_ref, c_ref, base):
    c_len, dk = z.shape
    e_abs = jnp.exp(-jnp.abs(z))
    inv = 1.0 / (1.0 + e_abs)
    sig = jnp.where(z >= 0, inv, e_abs * inv)
    sig_neg = jnp.where(z >= 0, e_abs * inv, inv)
    f = lb + (1.0 - lb) * sig
    log_sig = jnp.minimum(z, 0.0) + jnp.log(inv)
    log_f = jnp.maximum(jnp.log1p(-lb) + log_sig, jnp.log(jnp.maximum(f, 1e-30)))
    k_ref[pl.ds(base, c_len), :] = (1.0 - lb) * sig_neg

    hi = log_f.astype(BF16)
    rest = log_f - hi.astype(F32)
    mid = rest.astype(BF16)
    lo = (rest - mid.astype(F32)).astype(BF16)
    sums = _dot(cum, jnp.concatenate([hi, mid, lo], axis=1))
    c = sums[:, :dk] + sums[:, dk:2 * dk] + sums[:, 2 * dk:]
    c_ref[pl.ds(base, c_len), :] = c

    span = jnp.zeros((1, dk), F32)
    for g in range(0, c_len, SCAN_GROUP):
        ends = c_ref[pl.ds(base + g, 1), :] - c_ref[pl.ds(base + g + SCAN_GROUP - 1, 1), :]
        span = jnp.maximum(span, jnp.abs(ends))
    return span


def _scan_chunk(q, k, c, v, c_row, st_ref, code, n_levels, forward, factorised):
    c_len, dk = q.shape

    def reference_rows(block, offset):
        parts = [jnp.broadcast_to(c_row(n * block + offset), (block, dk)) for n in range(c_len // block)]
        return parts[0] if len(parts) == 1 else jnp.concatenate(parts, axis=0)

    att = jnp.zeros((c_len, c_len), F32)
    for lvl in range(1, n_levels + 1):
        half = c_len >> lvl
        e = jnp.exp(-jnp.abs(c - reference_rows(2 * half, half)))
        att = jnp.where(code == lvl, _dot_nt((q * e).astype(BF16), (k * e).astype(BF16)), att)

    if factorised:
        dq = c - reference_rows(SCAN_GROUP, 0 if forward else SCAN_GROUP - 1)
        same = _dot_nt((q * jnp.exp(dq)).astype(BF16), (k * jnp.exp(-dq)).astype(BF16))
    else:
        row = lax.broadcasted_iota(jnp.int32, (c_len, dk), 0)
        t = lax.broadcasted_iota(jnp.int32, (c_len, c_len), 0)
        s = lax.broadcasted_iota(jnp.int32, (c_len, c_len), 1)
        pos = row & (SCAN_GROUP - 1)
        same = jnp.zeros((c_len, c_len), F32)
        for dist in range(SCAN_GROUP):
            if dist == 0:
                k_s, c_s = k, c
            else:
                shift = dist if forward else c_len - dist
                k_s, c_s = pltpu.roll(k, shift, 0), pltpu.roll(c, shift, 0)
            valid = (pos >= dist) if forward else (pos + dist < SCAN_GROUP)
            term = q * k_s * jnp.exp(jnp.where(valid, c - c_s, 0.0))
            a = jnp.sum(jnp.where(valid, term, 0.0), axis=-1, keepdims=True)
            same = jnp.where(s == (t - dist if forward else t + dist), a, same)
    att = jnp.where(code == n_levels + 1, same, att)

    st = st_ref[...]
    o = _dot(att.astype(BF16), v.astype(BF16)) + _dot_nt((q * jnp.exp(c)).astype(BF16), st.astype(BF16))
    c_end = c_row(c_len - 1 if forward else 0)
    k_dec = (k * jnp.exp(c_end - c)).astype(BF16)
    st_ref[...] = st * jnp.exp(c_end) + _dot_tn(v.astype(BF16), k_dec)
    return o


def _hgrn_kernel(*refs, layer_j, chunk, n_levels, has_s0, n_prior, emit_state):
    q_ref, zf_ref, zb_ref, v_ref, g_ref, alb_ref, gn_ref, code_ref, sums_ref = refs[:9]
    n_in = 9 + int(has_s0) + n_prior
    o_ref = refs[n_in]
    s_ref, of_ref, ob_ref, k_ref, c_ref = refs[-5:]
    seq = q_ref.shape[0]
    n_chunks = seq // chunk

    def chunk_base(n):
        return pl.multiple_of(n * chunk, chunk)

    a = alb_ref[...]
    e = jnp.exp(a - jnp.max(a, axis=0, keepdims=True))
    p = e / jnp.sum(e, axis=0, keepdims=True)
    cum = p[0]
    for n in range(1, layer_j + 1):
        cum = cum + p[n]
    lb = jnp.clip(cum - p[0], 0.0, LB_MAX)

    def gates(n, span):
        base = chunk_base(n)
        rows = pl.ds(base, chunk)
        span_f = _gate_chunk(zf_ref[rows, :], lb[0:1], sums_ref[0], k_ref.at[0], c_ref.at[0], base)
        span_b = _gate_chunk(zb_ref[rows, :], lb[1:2], sums_ref[1], k_ref.at[1], c_ref.at[1], base)
        return jnp.maximum(span, jnp.maximum(span_f, span_b))

    unroll = 2 if n_chunks % 2 == 0 else 1
    span = lax.fori_loop(0, n_chunks, gates, jnp.zeros((1, q_ref.shape[1]), F32), unroll=unroll)
    safe = jnp.max(span) <= SAFE_EXP

    for direction in range(2):
        s_ref[direction] = refs[9][direction].T if has_s0 else jnp.zeros(s_ref.shape[1:], F32)

    def scan(factorised):
        def one(direction, base):
            rows = pl.ds(base, chunk)
            c_dir = c_ref.at[direction]
            return _scan_chunk(q_ref[rows, :], k_ref[direction, rows, :], c_dir[rows, :], v_ref[rows, :],
                               lambda r: c_dir[pl.ds(base + r, 1), :], s_ref.at[direction],
                               code_ref[direction], n_levels, direction == 0, factorised)

        def body(n, carry):
            base_f = chunk_base(n)
            base_b = chunk_base(n_chunks - 1 - n)
            of_ref[pl.ds(base_f, chunk), :] = one(0, base_f)
            ob_ref[pl.ds(base_b, chunk), :] = one(1, base_b)
            return carry

        lax.fori_loop(0, n_chunks, body, 0, unroll=unroll if factorised else 1)

    lax.cond(safe, lambda: scan(True), lambda: scan(False))

    o = of_ref[...] + ob_ref[...]
    o_ref[...] = _rms(o, gn_ref[...]) * _silu(g_ref[...])
    if emit_state:
        for direction in range(2):
            refs[n_in + 1][direction] = s_ref[direction].T


def _state_spec(layer_j):
    return pl.BlockSpec((None, None, 2, None, LANES, LANES), lambda b, h: (b, layer_j, 0, h, 0, 0))


def _hgrn(proj, row_block0, seq, n_batch, a_lb, gnorm, layer_j, s0=None, emit_state=False, prior_state=None):
    dk = LANES
    heads = A_HEADS
    chunk = min(SCAN_CHUNK, seq)
    codes, sums, n_levels = _scan_constants(chunk)
    n_layers = a_lb.shape[0]

    def col(section):
        return pl.BlockSpec((seq, dk), lambda b, h: (row_block0 + b, section * heads + h))

    def const(a):
        return pl.BlockSpec(a.shape, lambda b, h: (0,) * a.ndim)

    extra, extra_specs, aliases = [], [], {}
    if s0 is not None:
        extra.append(s0)
        extra_specs.append(_state_spec(layer_j))
    if prior_state is not None:
        aliases = {9 + len(extra): 1}
        extra.append(prior_state)
        extra_specs.append(pl.BlockSpec(memory_space=pl.ANY))
    out_shape = [jax.ShapeDtypeStruct((n_batch * seq, heads * dk), F32)]
    out_specs = [pl.BlockSpec((seq, dk), lambda b, h: (b, h))]
    if emit_state:
        out_shape.append(jax.ShapeDtypeStruct((n_batch, n_layers, 2, heads, dk, dk), F32))
        out_specs.append(_state_spec(layer_j))
    return pl.pallas_call(
        functools.partial(_hgrn_kernel, layer_j=layer_j, chunk=chunk, n_levels=n_levels, has_s0=s0 is not None,
                          n_prior=int(prior_state is not None), emit_state=emit_state),
        out_shape=tuple(out_shape),
        grid=(n_batch, heads),
        in_specs=[col(0), col(1), col(2), col(3), col(4),
                  pl.BlockSpec((n_layers, 2, dk), lambda b, h: (0, 0, h)),
                  pl.BlockSpec((1, dk), lambda b, h: (0, 0)), const(codes), const(sums)] + extra_specs,
        out_specs=tuple(out_specs),
        input_output_aliases=aliases,
        scratch_shapes=[pltpu.VMEM((2, dk, dk), F32), pltpu.VMEM((seq, dk), F32), pltpu.VMEM((seq, dk), F32),
                        pltpu.VMEM((2, seq, dk), F32), pltpu.VMEM((2, seq, dk), F32)],
        compiler_params=_params("parallel", "parallel"),
        name="hgrn2_scan",
    )(proj, proj, proj, proj, proj, a_lb, gnorm, codes, sums, *extra)


def _head_lanes(shape, head):
    lane = lax.broadcasted_iota(jnp.int32, shape, len(shape) - 1)
    half = shape[-1] // 2
    return (lane < half) if head == 0 else (lane >= half)


def _ctx_attn_kernel(*refs, scale):
    q_ref, k_ref, v_ref = refs[:3]
    o_ref, ko_ref, vo_ref = refs[-3:]
    dh = LANES // 2
    for pair in range(q_ref.shape[1] // LANES):
        lanes = slice(pair * LANES, (pair + 1) * LANES)
        k32 = k_ref[:, lanes]
        v32 = v_ref[:, lanes]
        k = k32.astype(BF16)
        v = v32.astype(BF16)
        q = q_ref[:, lanes]
        outs = []
        for head in range(2):
            ko_ref[2 * pair + head] = k32[:, head * dh:(head + 1) * dh]
            vo_ref[2 * pair + head] = v32[:, head * dh:(head + 1) * dh]
            qh = jnp.where(_head_lanes(q.shape, head), q, 0.0).astype(BF16)
            s = _dot_nt(qh, k) * scale
            p = jnp.exp(s - jnp.max(s, axis=-1, keepdims=True))
            outs.append(_dot(p.astype(BF16), v) / jnp.sum(p, axis=-1, keepdims=True))
        o_ref[:, lanes] = jnp.where(_head_lanes(q.shape, 0), outs[0], outs[1])


def _ctx_attention(proj, n_batch, seq, col0, n_layers, layer_j, caches):
    width = B_HEADS // 2 * LANES
    dh = LANES // 2
    cb = col0 // width
    assert col0 % width == 0
    cache = jax.ShapeDtypeStruct((n_batch, n_layers, B_HEADS, seq, dh), F32)
    cache_spec = pl.BlockSpec((None, None, B_HEADS, seq, dh), lambda b: (b, layer_j, 0, 0, 0))

    def col(section):
        return pl.BlockSpec((seq, width), lambda b: (b, cb + section))

    prior = () if caches is None else tuple(caches)
    return pl.pallas_call(
        functools.partial(_ctx_attn_kernel, scale=1.0 / math.sqrt(dh)),
        out_shape=(jax.ShapeDtypeStruct((n_batch * seq, width), F32), cache, cache),
        grid=(n_batch,),
        in_specs=[col(0), col(1), col(2)] + [pl.BlockSpec(memory_space=pl.ANY)] * len(prior),
        out_specs=(pl.BlockSpec((seq, width), lambda b: (b, 0)), cache_spec, cache_spec),
        input_output_aliases={3 + n: 1 + n for n in range(len(prior))},
        compiler_params=_params("parallel"),
        name="context_attention",
    )(proj, proj, proj, *prior)


def _rpb_tiles_kernel(rpb_ref, o_ref):
    h = pl.program_id(0)
    qc = lax.broadcasted_iota(jnp.int32, (GRID_W, GRID_W), 0)
    kc = lax.broadcasted_iota(jnp.int32, (GRID_W, GRID_W), 1)
    c0 = jnp.clip(qc - NA_COLS // 2, 0, GRID_W - NA_COLS)
    col_ok = (kc >= c0) & (kc < c0 + NA_COLS)
    dc = jnp.clip(kc - qc, -(NA_COLS - 1), NA_COLS - 1) + NA_COLS - 1
    for dr in range(2 * NA_ROWS - 1):
        t = jnp.zeros((GRID_W, GRID_W), F32)
        for i in range(2 * NA_COLS - 1):
            t = jnp.where(dc == i, rpb_ref[h, dr, i], t)
        o_ref[dr] = jnp.where(col_ok, t, MASKED)


def _rpb_tiles(rpb):
    heads, n_dr, n_dc = rpb.shape
    return pl.pallas_call(
        _rpb_tiles_kernel,
        out_shape=jax.ShapeDtypeStruct((heads, n_dr, GRID_W, GRID_W), F32),
        grid=(heads,),
        in_specs=[pl.BlockSpec(memory_space=pltpu.SMEM)],
        out_specs=pl.BlockSpec((None, n_dr, GRID_W, GRID_W), lambda h: (h, 0, 0, 0)),
        compiler_params=_params("parallel"),
        name="rpb_tiles",
    )(rpb)


def _nbr_geometry(grid_rows):
    win = min(NA_ROWS, grid_rows)
    blocks = []
    for qb in range(grid_rows // Q_ROWS):
        r0 = qb * Q_ROWS
        key0 = min(max(r0 - win // 2, 0), grid_rows - KEY_ROWS)
        tiles = []
        for i in range(Q_ROWS):
            r = r0 + i
            start = min(max(r - win // 2, 0), grid_rows - win)
            assert key0 <= start and start + win <= key0 + KEY_ROWS
            tiles.append([(key0 + jr - r + NA_ROWS - 1) if start <= key0 + jr < start + win else None
                          for jr in range(KEY_ROWS)])
        blocks.append((key0, tiles))
    return blocks


def _nbr_attn_kernel(q_ref, k_ref, v_ref, ck_ref, cv_ref, tiles_ref, o_ref, bias_ref, *, scale, geometry):
    w = GRID_W
    nq = Q_ROWS * w
    nk = KEY_ROWS * w

    @pl.when(pl.program_id(1) == 0)
    def _():
        masked = jnp.full((w, w), MASKED, F32)
        for head in range(2):
            for qb, (_, tiles) in enumerate(geometry):
                for i, row_tiles in enumerate(tiles):
                    for jr, dr in enumerate(row_tiles):
                        tile = masked if dr is None else tiles_ref[head, dr]
                        bias_ref[head, qb, i * w:(i + 1) * w, jr * w:(jr + 1) * w] = tile

    ck = ck_ref[...].astype(BF16)
    cv = cv_ref[...].astype(BF16)
    for qb, (key0, _) in enumerate(geometry):
        q = q_ref[qb * nq:(qb + 1) * nq, :]
        k = k_ref[key0 * w:key0 * w + nk, :].astype(BF16)
        v = v_ref[key0 * w:key0 * w + nk, :].astype(BF16)
        outs = []
        for head in range(2):
            qh = jnp.where(_head_lanes(q.shape, head), q, 0.0).astype(BF16)
            s_loc = _dot_nt(qh, k) * scale + bias_ref[head, qb]
            s_ctx = _dot_nt(qh, ck) * scale
            m = jnp.maximum(jnp.max(s_loc, axis=-1, keepdims=True), jnp.max(s_ctx, axis=-1, keepdims=True))
            p_loc = jnp.exp(s_loc - m)
            p_ctx = jnp.exp(s_ctx - m)
            denom = jnp.sum(p_loc, axis=-1, keepdims=True) + jnp.sum(p_ctx, axis=-1, keepdims=True)
            outs.append((_dot(p_loc.astype(BF16), v) + _dot(p_ctx.astype(BF16), cv)) / denom)
        o_ref[qb * nq:(qb + 1) * nq, :] = jnp.where(_head_lanes(q.shape, 0), outs[0], outs[1])


def _nbr_attention(proj, row_block0, seq, n_batch, col0, ck, cv, tiles):
    pairs = B_HEADS // 2
    cb = col0 // LANES
    past = ck.shape[2]
    grid_rows = seq // GRID_W
    assert grid_rows % Q_ROWS == 0 and grid_rows >= KEY_ROWS
    geometry = _nbr_geometry(grid_rows)

    def col(section):
        return pl.BlockSpec((seq, LANES), lambda h, b: (row_block0 + b, cb + section * pairs + h))

    return pl.pallas_call(
        functools.partial(_nbr_attn_kernel, scale=1.0 / math.sqrt(LANES // 2), geometry=geometry),
        out_shape=jax.ShapeDtypeStruct((n_batch * seq, pairs * LANES), F32),
        grid=(pairs, n_batch),
        in_specs=[col(0), col(1), col(2),
                  pl.BlockSpec((None, None, past, LANES), lambda h, b: (b, h, 0, 0)),
                  pl.BlockSpec((None, None, past, LANES), lambda h, b: (b, h, 0, 0)),
                  pl.BlockSpec((2,) + tiles.shape[1:], lambda h, b: (h, 0, 0, 0))],
        out_specs=pl.BlockSpec((seq, LANES), lambda h, b: (b, h)),
        scratch_shapes=[pltpu.VMEM((2, len(geometry), Q_ROWS * GRID_W, KEY_ROWS * GRID_W), F32)],
        compiler_params=_params("arbitrary", "arbitrary"),
        name="neighbourhood_attention",
    )(proj, proj, proj, ck, cv, tiles)


def _head_pairs(t):
    b, h, l, dh = t.shape
    return t.reshape(b, h // 2, 2, l, dh).transpose(0, 1, 3, 2, 4).reshape(b, h // 2, l, 2 * dh)


def kernel(x_prompt, x_sample, c, state_hgrn, cache_k, cache_v, c_ctx, mod_w, mod_b, norm_pre, norm_post,
           ffn_w_in, ffn_w_out, ab_w_in, ab_w_out, a_lb, a_gnorm, b_rpb, cv_w_in, cv_conv, cv_w_out):
    n_ctx, ctx_len, d = x_prompt.shape
    n_lat, lat_len, _ = x_sample.shape
    depth = mod_w.shape[0]
    d_a = a_lb.shape[-1]
    rows = _Rows(n_ctx * ctx_len, ctx_len, n_lat * lat_len, lat_len, ROW_TILE)
    assert (n_ctx * ctx_len) % lat_len == 0
    lat_block0 = (n_ctx * ctx_len) // lat_len

    x = (x_prompt.reshape(-1, d), x_sample.reshape(-1, d))

    cond = jnp.concatenate([c_ctx[None, :], c], axis=0)
    cond = jnp.pad(cond, ((0, -cond.shape[0] % 8), (0, 0)))
    mod = _modulation(cond, mod_w, mod_b)
    mod = mod.reshape(depth, cond.shape[0], 1, N_MOD * d)

    ffn_w_in = ffn_w_in.astype(BF16)
    ffn_w_out = ffn_w_out.astype(BF16)
    ab_w_in = ab_w_in.astype(BF16)
    ab_w_out = ab_w_out.astype(BF16)
    cv_w_in = cv_w_in.astype(BF16)
    cv_w_out = cv_w_out.astype(BF16)

    n_even = a_lb.shape[0]
    new_state, new_caches = None, None
    for l in range(depth):
        j = l // 2

        def gains(t, sub):
            return t[l, sub][None, :]

        x = _ffn(x, mod[l], gains(norm_pre, 0), gains(norm_post, 0), (ffn_w_in, (l, 0)), (ffn_w_out, (l, 0)),
                 rows, 0)
        if l % 2 == 0:
            proj = _inproj(x, mod[l], gains(norm_pre, 1), (ab_w_in, (j,)), rows)
            gn = a_gnorm[j][None, :]
            oa_ctx, new_state = _hgrn(proj, 0, ctx_len, n_ctx, a_lb, gn, j, emit_state=True,
                                      prior_state=new_state)
            oa_lat, = _hgrn(proj, lat_block0, lat_len, n_lat, a_lb, gn, j, s0=state_hgrn)
            ob_ctx, *new_caches = _ctx_attention(proj, n_ctx, ctx_len, 5 * d_a, n_even, j, new_caches)
            ob_lat = _nbr_attention(proj, lat_block0, lat_len, n_lat, 5 * d_a,
                                    _head_pairs(cache_k[:, j]), _head_pairs(cache_v[:, j]),
                                    _rpb_tiles(b_rpb[j]))
            x = _outproj(x, (oa_ctx, oa_lat), (ob_ctx, ob_lat), mod[l], gains(norm_post, 1), (ab_w_out, (j,)),
                         rows)
        else:
            x = _conv_mixer(x, mod[l], gains(norm_pre, 1), gains(norm_post, 1), (cv_w_in, (j,)), cv_conv[j],
                            (cv_w_out, (j,)), rows)
        x = _ffn(x, mod[l], gains(norm_pre, 2), gains(norm_post, 2), (ffn_w_in, (l, 1)), (ffn_w_out, (l, 1)),
                 rows, 2, split_out=(l == depth - 1))

    return (x[0].reshape(x_prompt.shape), x[1].reshape(x_sample.shape), new_state, *new_caches)
```

```python
import functools
import math

import jax
import jax.numpy as jnp
import numpy as np
from jax import lax
from jax.experimental import pallas as pl
from jax.experimental.pallas import tpu as pltpu

F32 = jnp.float32
BF16 = jnp.bfloat16

EPS = 1e-6
LB_MAX = 1.0 - 1e-4
MASKED = -1e30

GRID_W = 64
A_HEADS = 4
B_HEADS = 8
NA_ROWS = 8
NA_COLS = 16
CONV_W = 3
N_MOD = 9

VMEM_LIMIT_BYTES = 56 * 1024 * 1024
LANES = 128

ROW_TILE = 512
FFN_ROW_TILE = 1024
FFN_SUB_TILES = 4
HALO = 16
SCAN_CHUNK = 128
SCAN_GROUP = 16
SCAN_HEADS_PER_STEP = 2
SAFE_EXP2 = 115.0
LOG2_E = math.log2(math.e)
Q_ROWS = 8
KEY_ROWS = 16


def _params(*sem):
    return pltpu.CompilerParams(dimension_semantics=sem, vmem_limit_bytes=VMEM_LIMIT_BYTES)


def _rms(x, g):
    return x * lax.rsqrt(jnp.mean(x * x, axis=-1, keepdims=True) + EPS) * g


def _silu(x):
    return x * jax.nn.sigmoid(x)


def _dot(a, b):
    return jnp.dot(a, b, preferred_element_type=F32)


def _dot_nt(a, b):
    return lax.dot_general(a, b, (((1,), (1,)), ((), ())), preferred_element_type=F32)


def _dot_tn(a, b):
    return lax.dot_general(a, b, (((0,), (0,)), ((), ())), preferred_element_type=F32)


def _resident_spec(w, index):
    lead = len(index)
    zeros = (0,) * (w.ndim - lead)
    return pl.BlockSpec((None,) * lead + tuple(w.shape[lead:]), lambda *_: tuple(index) + zeros,
                        pipeline_mode=pl.Buffered(1))


def _mod_parts(mod, j, d):
    return (mod[:, (3 * j) * d:(3 * j + 1) * d],
            mod[:, (3 * j + 1) * d:(3 * j + 2) * d],
            mod[:, (3 * j + 2) * d:(3 * j + 3) * d])


class _Rows:
    def __init__(self, n_ctx_rows, ctx_len, n_lat_rows, lat_len, tile):
        assert ctx_len % tile == 0 or tile % ctx_len == 0
        assert lat_len % tile == 0 and n_ctx_rows % tile == 0 and n_lat_rows % tile == 0
        self.tile = tile
        self.n_ctx_tiles = n_ctx_rows // tile
        self.n_tiles = (n_ctx_rows + n_lat_rows) // tile
        self.lat_tiles_per_batch = lat_len // tile
        self.ctx_len = ctx_len
        self.lat_len = lat_len

    def mod_row(self, i):
        return jnp.where(i < self.n_ctx_tiles, 0, 1 + (i - self.n_ctx_tiles) // self.lat_tiles_per_batch)

    def group_specs(self, width):
        nc = self.n_ctx_tiles
        return [pl.BlockSpec((self.tile, width), lambda i: (jnp.minimum(i, nc - 1), 0)),
                pl.BlockSpec((self.tile, width), lambda i: (jnp.maximum(i - nc, 0), 0))]

    def mod_spec(self, width):
        return pl.BlockSpec((None, 1, width), lambda i: (self.mod_row(i), 0, 0))


def _mod_kernel(cond_ref, w_ref, b_ref, o_ref):
    s = _silu(cond_ref[...]).astype(BF16)
    o_ref[...] = _dot(s, w_ref[...].astype(BF16)) + b_ref[...]


def _modulation(cond, mod_w, mod_b):
    depth, d, n = mod_w.shape
    rows = cond.shape[0]
    tn = 1536
    assert n % tn == 0
    return pl.pallas_call(
        _mod_kernel,
        out_shape=jax.ShapeDtypeStruct((depth, rows, n), F32),
        grid=(depth, n // tn),
        in_specs=[pl.BlockSpec((rows, d), lambda l, j: (0, 0)),
                  pl.BlockSpec((None, d, tn), lambda l, j: (l, 0, j)),
                  pl.BlockSpec((None, 1, tn), lambda l, j: (l, 0, j))],
        out_specs=pl.BlockSpec((None, rows, tn), lambda l, j: (l, 0, j)),
        compiler_params=_params("parallel", "parallel"),
        name="adaln_mod",
    )(cond, mod_w, mod_b.reshape(depth, 1, n))


def _ffn_kernel(*refs, j, tf, n_sub, n_ctx_tiles, split_in, split_out):
    n_x = 2 if split_in else 1
    x_refs = refs[:n_x]
    mod_ref, gpre_ref, gpost_ref, win_ref, wout_ref = refs[n_x:n_x + 5]
    o_refs = refs[n_x + 5:-2]
    h_ref, act_ref = refs[-2:]
    is_ctx = pl.program_id(0) < n_ctx_tiles

    def load_x(rows):
        if split_in:
            return jnp.where(is_ctx, x_refs[0][rows, :], x_refs[1][rows, :])
        return x_refs[0][rows, :]

    tm, d = x_refs[0].shape
    dff = wout_ref.shape[0]
    shift, scale, gate = _mod_parts(mod_ref[...], j, d)
    for sub in range(n_sub):
        rows = slice(sub * (tm // n_sub), (sub + 1) * (tm // n_sub))
        h_ref[rows, :] = (_rms(load_x(rows), gpre_ref[...]) * (1.0 + scale) + shift).astype(BF16)
        for c in range(dff // tf):
            g = _dot(h_ref[rows, :], win_ref[:, c * tf:(c + 1) * tf])
            u = _dot(h_ref[rows, :], win_ref[:, dff + c * tf:dff + (c + 1) * tf])
            act_ref[rows, c * tf:(c + 1) * tf] = (_silu(g) * u).astype(BF16)
        out = _dot(act_ref[rows, :], wout_ref[...])
        y = load_x(rows) + (0.5 * gate) * _rms(out, gpost_ref[...])
        if split_out:
            @pl.when(is_ctx)
            def _():
                o_refs[0][rows, :] = y

            @pl.when(jnp.logical_not(is_ctx))
            def _():
                o_refs[1][rows, :] = y
        else:
            o_refs[0][rows, :] = y


def _ffn(x, mod_l, g_pre, g_post, w_in, w_out, rows, j, split_out=False):
    split_in = isinstance(x, tuple)
    d = x[0].shape[1] if split_in else x.shape[1]
    dff = w_out[0].shape[-2]
    tm = rows.tile
    tf = 256
    assert dff % tf == 0
    out_shape = jax.ShapeDtypeStruct((rows.n_tiles * tm, d), F32)
    out_spec = pl.BlockSpec((tm, d), lambda i: (i, 0))
    if split_out:
        out_shape = (jax.ShapeDtypeStruct((rows.n_ctx_tiles * tm, d), F32),
                     jax.ShapeDtypeStruct(((rows.n_tiles - rows.n_ctx_tiles) * tm, d), F32))
        out_spec = tuple(rows.group_specs(d))
    return pl.pallas_call(
        functools.partial(_ffn_kernel, j=j, tf=tf, n_sub=FFN_SUB_TILES, n_ctx_tiles=rows.n_ctx_tiles,
                          split_in=split_in, split_out=split_out),
        out_shape=out_shape,
        grid=(rows.n_tiles,),
        in_specs=(rows.group_specs(d) if split_in else [pl.BlockSpec((tm, d), lambda i: (i, 0))]) + [
                  rows.mod_spec(mod_l.shape[-1]),
                  pl.BlockSpec((1, d), lambda i: (0, 0)),
                  pl.BlockSpec((1, d), lambda i: (0, 0)),
                  _resident_spec(*w_in), _resident_spec(*w_out)],
        out_specs=out_spec,
        scratch_shapes=[pltpu.VMEM((tm, d), BF16), pltpu.VMEM((tm, dff), BF16)],
        compiler_params=_params("arbitrary" if split_out else "parallel"),
        name="swiglu_sublayer",
    )(*(x if split_in else (x,)), mod_l, g_pre, g_post, w_in[0], w_out[0])


def _inproj_kernel(x_ref, mod_ref, gpre_ref, w_ref, o_ref, h_ref, *, tn):
    d = x_ref.shape[1]
    shift, scale, _ = _mod_parts(mod_ref[...], 1, d)
    h_ref[...] = (_rms(x_ref[...], gpre_ref[...]) * (1.0 + scale) + shift).astype(BF16)
    for c in range(w_ref.shape[1] // tn):
        o_ref[:, c * tn:(c + 1) * tn] = _dot(h_ref[...], w_ref[:, c * tn:(c + 1) * tn])


def _inproj(x, mod_l, g_pre, w, rows):
    m, d = x.shape
    n = w[0].shape[-1]
    tm = rows.tile
    tn = 512
    assert n % tn == 0
    return pl.pallas_call(
        functools.partial(_inproj_kernel, tn=tn),
        out_shape=jax.ShapeDtypeStruct((m, n), F32),
        grid=(rows.n_tiles,),
        in_specs=[pl.BlockSpec((tm, d), lambda i: (i, 0)),
                  rows.mod_spec(mod_l.shape[-1]),
                  pl.BlockSpec((1, d), lambda i: (0, 0)),
                  _resident_spec(*w)],
        out_specs=pl.BlockSpec((tm, n), lambda i: (i, 0)),
        scratch_shapes=[pltpu.VMEM((tm, d), BF16)],
        compiler_params=_params("parallel"),
        name="mixer_in_proj",
    )(x, mod_l, g_pre, w[0])


def _outproj_kernel(x_ref, ac_ref, al_ref, bc_ref, bl_ref, mod_ref, gpost_ref, w_ref, o_ref, *, n_ctx_tiles):
    d = x_ref.shape[1]
    da = ac_ref.shape[1]
    is_ctx = pl.program_id(0) < n_ctx_tiles
    _, _, gate = _mod_parts(mod_ref[...], 1, d)
    a = jnp.where(is_ctx, ac_ref[...], al_ref[...]).astype(BF16)
    b = jnp.where(is_ctx, bc_ref[...], bl_ref[...]).astype(BF16)
    out = _dot(a, w_ref[:da, :]) + _dot(b, w_ref[da:, :])
    o_ref[...] = x_ref[...] + gate * _rms(out, gpost_ref[...])


def _outproj(x, o_a, o_b, mod_l, g_post, w, rows):
    m, d = x.shape
    tm = rows.tile
    return pl.pallas_call(
        functools.partial(_outproj_kernel, n_ctx_tiles=rows.n_ctx_tiles),
        out_shape=jax.ShapeDtypeStruct((m, d), F32),
        grid=(rows.n_tiles,),
        in_specs=[pl.BlockSpec((tm, d), lambda i: (i, 0))]
                 + rows.group_specs(o_a[0].shape[1]) + rows.group_specs(o_b[0].shape[1]) + [
                  rows.mod_spec(mod_l.shape[-1]),
                  pl.BlockSpec((1, d), lambda i: (0, 0)),
                  _resident_spec(*w)],
        out_specs=pl.BlockSpec((tm, d), lambda i: (i, 0)),
        compiler_params=_params("parallel"),
        name="mixer_out_proj",
    )(x, *o_a, *o_b, mod_l, g_post, w[0])


def _conv_kernel(x_ref, xp_ref, xn_ref, mod_ref, gpre_ref, gpost_ref, win_ref, cw_ref, wout_ref, o_ref,
                 h_ref, u_ref, z_ref, *, tc, rows):
    tm, d = x_ref.shape
    i = pl.program_id(0)
    shift, scale, gate = _mod_parts(mod_ref[...], 1, d)
    g_pre = gpre_ref[...]

    def pre(x):
        return (_rms(x, g_pre) * (1.0 + scale) + shift).astype(BF16)

    h_ref[0:HALO, :] = pre(xp_ref[...])
    h_ref[HALO:HALO + tm, :] = pre(x_ref[...])
    h_ref[HALO + tm:, :] = pre(xn_ref[...])

    seq_len = jnp.where(i < rows.n_ctx_tiles, rows.ctx_len, rows.lat_len)
    pos = (i * tm + lax.broadcasted_iota(jnp.int32, (tm, 1), 0)) & (seq_len - 1)
    has_left = pos != 0
    has_right = pos != seq_len - 1

    for c in range(d // tc):
        cols = slice(c * tc, (c + 1) * tc)
        b_gate = _dot(h_ref[HALO:HALO + tm, :], win_ref[:, c * tc:(c + 1) * tc])
        c_gate = _dot(h_ref[...], win_ref[:, d + c * tc:d + (c + 1) * tc])
        x_in = _dot(h_ref[...], win_ref[:, 2 * d + c * tc:2 * d + (c + 1) * tc])
        u_ref[...] = c_gate * x_in
        cw = cw_ref[:, cols]
        y = (cw[0:1, :] * jnp.where(has_left, u_ref[HALO - 1:HALO - 1 + tm, :], 0.0)
             + cw[1:2, :] * u_ref[HALO:HALO + tm, :]
             + cw[2:3, :] * jnp.where(has_right, u_ref[HALO + 1:HALO + 1 + tm, :], 0.0))
        z_ref[:, cols] = (b_gate * y).astype(BF16)
    out = _dot(z_ref[...], wout_ref[...])
    o_ref[...] = x_ref[...] + gate * _rms(out, gpost_ref[...])


def _conv_mixer(x, mod_l, g_pre, g_post, w_in, conv_w, w_out, rows):
    m, d = x.shape
    tm = rows.tile
    tc = 256
    assert rows.ctx_len & (rows.ctx_len - 1) == 0 and rows.lat_len & (rows.lat_len - 1) == 0
    hb = tm // HALO
    last = m // HALO - 1
    return pl.pallas_call(
        functools.partial(_conv_kernel, tc=tc, rows=rows),
        out_shape=jax.ShapeDtypeStruct((m, d), F32),
        grid=(rows.n_tiles,),
        in_specs=[pl.BlockSpec((tm, d), lambda i: (i, 0)),
                  pl.BlockSpec((HALO, d), lambda i: (jnp.maximum(i * hb - 1, 0), 0)),
                  pl.BlockSpec((HALO, d), lambda i: (jnp.minimum((i + 1) * hb, last), 0)),
                  rows.mod_spec(mod_l.shape[-1]),
                  pl.BlockSpec((1, d), lambda i: (0, 0)),
                  pl.BlockSpec((1, d), lambda i: (0, 0)),
                  _resident_spec(*w_in),
                  pl.BlockSpec((CONV_W, d), lambda i: (0, 0)),
                  _resident_spec(*w_out)],
        out_specs=pl.BlockSpec((tm, d), lambda i: (i, 0)),
        scratch_shapes=[pltpu.VMEM((tm + 2 * HALO, d), BF16),
                        pltpu.VMEM((tm + 2 * HALO, tc), F32),
                        pltpu.VMEM((tm, d), BF16)],
        compiler_params=_params("parallel"),
        name="conv_mixer",
    )(x, x, x, mod_l, g_pre, g_post, w_in[0], conv_w, w_out[0])


def _scan_constants(c):
    t, s = np.meshgrid(np.arange(c), np.arange(c), indexing="ij")
    x = t ^ s
    n_levels = int(math.log2(c // SCAN_GROUP))
    code = np.full((c, c), n_levels, np.int32)
    for lvl in range(n_levels - 1, 0, -1):
        code = np.where(x >= (c >> lvl), lvl, code)
    code = np.where(x < SCAN_GROUP, n_levels + 1, code)
    visited = np.stack([s <= t, s >= t])
    codes = np.where(visited, code[None], 0).astype(np.int32)
    return jnp.asarray(codes), jnp.asarray(visited, BF16), n_levels


def _gate_chunk(z, lb, cum, k_ref, c_ref, base):
    c_len, dk = z.shape
    e_abs = jnp.exp(-jnp.abs(z))
    inv = 1.0 / (1.0 + e_abs)
    sig = jnp.where(z >= 0, inv, e_abs * inv)
    sig_neg = jnp.where(z >= 0, e_abs * inv, inv)
    f = lb + (1.0 - lb) * sig
    log_sig = jnp.minimum(z, 0.0) + jnp.log(inv)
    log_f = jnp.maximum(jnp.log1p(-lb) + log_sig, jnp.log(jnp.maximum(f, 1e-30))) * LOG2_E
    k_ref[pl.ds(base, c_len), :] = (1.0 - lb) * sig_neg

    hi = log_f.astype(BF16)
    rest = log_f - hi.astype(F32)
    mid = rest.astype(BF16)
    lo = (rest - mid.astype(F32)).astype(BF16)
    sums = _dot(cum, jnp.concatenate([hi, mid, lo], axis=1))
    c = sums[:, :dk] + sums[:, dk:2 * dk] + sums[:, 2 * dk:]
    c_ref[pl.ds(base, c_len), :] = c

    span = jnp.zeros((1, dk), F32)
    for g in range(0, c_len, SCAN_GROUP):
        ends = c_ref[pl.ds(base + g, 1), :] - c_ref[pl.ds(base + g + SCAN_GROUP - 1, 1), :]
        span = jnp.maximum(span, jnp.abs(ends))
    return span


def _scan_chunk(q, k, c, v, c_row, st_ref, code, n_levels, forward, factorised):
    c_len, dk = q.shape
    q16 = q.astype(BF16)
    k16 = k.astype(BF16)

    def reference_rows(block, offset):
        parts = [jnp.broadcast_to(c_row(n * block + offset), (block, dk)) for n in range(c_len // block)]
        return parts[0] if len(parts) == 1 else jnp.concatenate(parts, axis=0)

    att = jnp.zeros((c_len, c_len), F32)
    for lvl in range(1, n_levels + 1):
        half = c_len >> lvl
        e = jnp.exp2(-jnp.abs(c - reference_rows(2 * half, half))).astype(BF16)
        att = jnp.where(code == lvl, _dot_nt(q16 * e, k16 * e), att)

    if factorised:
        dq = c - reference_rows(SCAN_GROUP, 0 if forward else SCAN_GROUP - 1)
        same = _dot_nt(q16 * jnp.exp2(dq).astype(BF16), k16 * jnp.exp2(-dq).astype(BF16))
    else:
        row = lax.broadcasted_iota(jnp.int32, (c_len, dk), 0)
        t = lax.broadcasted_iota(jnp.int32, (c_len, c_len), 0)
        s = lax.broadcasted_iota(jnp.int32, (c_len, c_len), 1)
        pos = row & (SCAN_GROUP - 1)
        same = jnp.zeros((c_len, c_len), F32)
        for dist in range(SCAN_GROUP):
            if dist == 0:
                k_s, c_s = k, c
            else:
                shift = dist if forward else c_len - dist
                k_s, c_s = pltpu.roll(k, shift, 0), pltpu.roll(c, shift, 0)
            valid = (pos >= dist) if forward else (pos + dist < SCAN_GROUP)
            term = q * k_s * jnp.exp2(jnp.where(valid, c - c_s, 0.0))
            a = jnp.sum(jnp.where(valid, term, 0.0), axis=-1, keepdims=True)
            same = jnp.where(s == (t - dist if forward else t + dist), a, same)
    att = jnp.where(code == n_levels + 1, same, att)

    st = st_ref[...]
    v16 = v.astype(BF16)
    o = _dot(att.astype(BF16), v16) + _dot_nt(q16 * jnp.exp2(c).astype(BF16), st.astype(BF16))
    c_end = c_row(c_len - 1 if forward else 0)
    st_ref[...] = st * jnp.exp2(c_end) + _dot_tn(v16, k16 * jnp.exp2(c_end - c).astype(BF16))
    return o


def _hgrn_kernel(*refs, layer_j, chunk, n_levels, heads, has_s0, n_prior, emit_state):
    q_ref, zf_ref, zb_ref, v_ref, g_ref, alb_ref, gn_ref, code_ref, sums_ref = refs[:9]
    n_in = 9 + int(has_s0) + n_prior
    o_ref = refs[n_in]
    s_ref, o_dir_ref, k_ref, c_ref = refs[-4:]
    seq = q_ref.shape[0]
    dk = q_ref.shape[1] // heads
    n_chunks = seq // chunk
    z_refs = (zf_ref, zb_ref)

    def chunk_base(n):
        return pl.multiple_of(n * chunk, chunk)

    def lanes(head):
        return slice(head * dk, (head + 1) * dk)

    a = alb_ref[...]
    e = jnp.exp(a - jnp.max(a, axis=0, keepdims=True))
    p = e / jnp.sum(e, axis=0, keepdims=True)
    cum = p[0]
    for n in range(1, layer_j + 1):
        cum = cum + p[n]
    lb = jnp.clip(cum - p[0], 0.0, LB_MAX)

    def gates(n, span):
        base = chunk_base(n)
        rows = pl.ds(base, chunk)
        for head in range(heads):
            for direction in range(2):
                span = jnp.maximum(span, _gate_chunk(
                    z_refs[direction][rows, lanes(head)], lb[direction:direction + 1, lanes(head)],
                    sums_ref[direction], k_ref.at[direction, head], c_ref.at[direction, head], base))
        return span

    unroll = 2 if n_chunks % 2 == 0 else 1
    span = lax.fori_loop(0, n_chunks, gates, jnp.zeros((1, dk), F32), unroll=unroll)
    safe = jnp.max(span) <= SAFE_EXP2

    for head in range(heads):
        for direction in range(2):
            s_ref[direction, head] = refs[9][direction, head].T if has_s0 else jnp.zeros((dk, dk), F32)

    def scan(factorised):
        def body(n, carry):
            for head in range(heads):
                for direction in range(2):
                    base = chunk_base(n if direction == 0 else n_chunks - 1 - n)
                    rows = pl.ds(base, chunk)
                    c_dir = c_ref.at[direction, head]
                    o_dir_ref[direction, head, rows, :] = _scan_chunk(
                        q_ref[rows, lanes(head)], k_ref[direction, head, rows, :], c_dir[rows, :],
                        v_ref[rows, lanes(head)], lambda r, c_dir=c_dir, base=base: c_dir[pl.ds(base + r, 1), :],
                        s_ref.at[direction, head], code_ref[direction], n_levels, direction == 0, factorised)
            return carry

        lax.fori_loop(0, n_chunks, body, 0, unroll=unroll if factorised else 1)

    lax.cond(safe, lambda: scan(True), lambda: scan(False))

    for head in range(heads):
        o = o_dir_ref[0, head] + o_dir_ref[1, head]
        o_ref[:, lanes(head)] = _rms(o, gn_ref[...]) * _silu(g_ref[:, lanes(head)])
        if emit_state:
            for direction in range(2):
                refs[n_in + 1][direction, head] = s_ref[direction, head].T


def _state_spec(layer_j, heads):
    return pl.BlockSpec((None, None, 2, heads, LANES, LANES), lambda b, h: (b, layer_j, 0, h, 0, 0))


def _hgrn(proj, row_block0, seq, n_batch, a_lb, gnorm, layer_j, s0=None, emit_state=False, prior_state=None):
    dk = LANES
    heads = SCAN_HEADS_PER_STEP
    groups = A_HEADS // heads
    chunk = min(SCAN_CHUNK, seq)
    codes, sums, n_levels = _scan_constants(chunk)
    n_layers = a_lb.shape[0]

    def col(section):
        return pl.BlockSpec((seq, heads * dk), lambda b, h: (row_block0 + b, section * groups + h))

    def const(a):
        return pl.BlockSpec(a.shape, lambda b, h: (0,) * a.ndim)

    extra, extra_specs, aliases = [], [], {}
    if s0 is not None:
        extra.append(s0)
        extra_specs.append(_state_spec(layer_j, heads))
    if prior_state is not None:
        aliases = {9 + len(extra): 1}
        extra.append(prior_state)
        extra_specs.append(pl.BlockSpec(memory_space=pl.ANY))
    out_shape = [jax.ShapeDtypeStruct((n_batch * seq, A_HEADS * dk), F32)]
    out_specs = [pl.BlockSpec((seq, heads * dk), lambda b, h: (b, h))]
    if emit_state:
        out_shape.append(jax.ShapeDtypeStruct((n_batch, n_layers, 2, A_HEADS, dk, dk), F32))
        out_specs.append(_state_spec(layer_j, heads))
    per_head = (2, heads, seq, dk)
    return pl.pallas_call(
        functools.partial(_hgrn_kernel, layer_j=layer_j, chunk=chunk, n_levels=n_levels, heads=heads,
                          has_s0=s0 is not None, n_prior=int(prior_state is not None), emit_state=emit_state),
        out_shape=tuple(out_shape),
        grid=(n_batch, groups),
        in_specs=[col(0), col(1), col(2), col(3), col(4),
                  pl.BlockSpec((n_layers, 2, heads * dk), lambda b, h: (0, 0, h)),
                  pl.BlockSpec((1, dk), lambda b, h: (0, 0)), const(codes), const(sums)] + extra_specs,
        out_specs=tuple(out_specs),
        input_output_aliases=aliases,
        scratch_shapes=[pltpu.VMEM((2, heads, dk, dk), F32), pltpu.VMEM(per_head, F32),
                        pltpu.VMEM(per_head, F32), pltpu.VMEM(per_head, F32)],
        compiler_params=_params("parallel", "parallel"),
        name="hgrn2_scan",
    )(proj, proj, proj, proj, proj, a_lb, gnorm, codes, sums, *extra)


def _head_lanes(shape, head):
    lane = lax.broadcasted_iota(jnp.int32, shape, len(shape) - 1)
    half = shape[-1] // 2
    return (lane < half) if head == 0 else (lane >= half)


def _ctx_attn_kernel(*refs, scale):
    q_ref, k_ref, v_ref = refs[:3]
    o_ref, ko_ref, vo_ref = refs[-3:]
    dh = LANES // 2
    for pair in range(q_ref.shape[1] // LANES):
        lanes = slice(pair * LANES, (pair + 1) * LANES)
        k32 = k_ref[:, lanes]
        v32 = v_ref[:, lanes]
        k = k32.astype(BF16)
        v = v32.astype(BF16)
        q = q_ref[:, lanes]
        outs = []
        for head in range(2):
            ko_ref[2 * pair + head] = k32[:, head * dh:(head + 1) * dh]
            vo_ref[2 * pair + head] = v32[:, head * dh:(head + 1) * dh]
            qh = jnp.where(_head_lanes(q.shape, head), q, 0.0).astype(BF16)
            s = _dot_nt(qh, k) * scale
            p = jnp.exp(s - jnp.max(s, axis=-1, keepdims=True))
            outs.append(_dot(p.astype(BF16), v) / jnp.sum(p, axis=-1, keepdims=True))
        o_ref[:, lanes] = jnp.where(_head_lanes(q.shape, 0), outs[0], outs[1])


def _ctx_attention(proj, n_batch, seq, col0, n_layers, layer_j, caches):
    width = B_HEADS // 2 * LANES
    dh = LANES // 2
    cb = col0 // width
    assert col0 % width == 0
    cache = jax.ShapeDtypeStruct((n_batch, n_layers, B_HEADS, seq, dh), F32)
    cache_spec = pl.BlockSpec((None, None, B_HEADS, seq, dh), lambda b: (b, layer_j, 0, 0, 0))

    def col(section):
        return pl.BlockSpec((seq, width), lambda b: (b, cb + section))

    prior = () if caches is None else tuple(caches)
    return pl.pallas_call(
        functools.partial(_ctx_attn_kernel, scale=1.0 / math.sqrt(dh)),
        out_shape=(jax.ShapeDtypeStruct((n_batch * seq, width), F32), cache, cache),
        grid=(n_batch,),
        in_specs=[col(0), col(1), col(2)] + [pl.BlockSpec(memory_space=pl.ANY)] * len(prior),
        out_specs=(pl.BlockSpec((seq, width), lambda b: (b, 0)), cache_spec, cache_spec),
        input_output_aliases={3 + n: 1 + n for n in range(len(prior))},
        compiler_params=_params("parallel"),
        name="context_attention",
    )(proj, proj, proj, *prior)


def _rpb_tiles_kernel(rpb_ref, o_ref):
    h = pl.program_id(0)
    qc = lax.broadcasted_iota(jnp.int32, (GRID_W, GRID_W), 0)
    kc = lax.broadcasted_iota(jnp.int32, (GRID_W, GRID_W), 1)
    c0 = jnp.clip(qc - NA_COLS // 2, 0, GRID_W - NA_COLS)
    col_ok = (kc >= c0) & (kc < c0 + NA_COLS)
    dc = jnp.clip(kc - qc, -(NA_COLS - 1), NA_COLS - 1) + NA_COLS - 1
    for dr in range(2 * NA_ROWS - 1):
        t = jnp.zeros((GRID_W, GRID_W), F32)
        for i in range(2 * NA_COLS - 1):
            t = jnp.where(dc == i, rpb_ref[h, dr, i], t)
        o_ref[dr] = jnp.where(col_ok, t, MASKED)


def _rpb_tiles(rpb):
    heads, n_dr, n_dc = rpb.shape
    return pl.pallas_call(
        _rpb_tiles_kernel,
        out_shape=jax.ShapeDtypeStruct((heads, n_dr, GRID_W, GRID_W), F32),
        grid=(heads,),
        in_specs=[pl.BlockSpec(memory_space=pltpu.SMEM)],
        out_specs=pl.BlockSpec((None, n_dr, GRID_W, GRID_W), lambda h: (h, 0, 0, 0)),
        compiler_params=_params("parallel"),
        name="rpb_tiles",
    )(rpb)


def _nbr_geometry(grid_rows):
    win = min(NA_ROWS, grid_rows)
    blocks = []
    for qb in range(grid_rows // Q_ROWS):
        r0 = qb * Q_ROWS
        key0 = min(max(r0 - win // 2, 0), grid_rows - KEY_ROWS)
        tiles = []
        for i in range(Q_ROWS):
            r = r0 + i
            start = min(max(r - win // 2, 0), grid_rows - win)
            assert key0 <= start and start + win <= key0 + KEY_ROWS
            tiles.append([(key0 + jr - r + NA_ROWS - 1) if start <= key0 + jr < start + win else None
                          for jr in range(KEY_ROWS)])
        blocks.append((key0, tiles))
    return blocks


def _nbr_attn_kernel(q_ref, k_ref, v_ref, ck_ref, cv_ref, tiles_ref, o_ref, bias_ref, *, scale, geometry):
    w = GRID_W
    nq = Q_ROWS * w
    nk = KEY_ROWS * w

    @pl.when(pl.program_id(1) == 0)
    def _():
        masked = jnp.full((w, w), MASKED, F32)
        for head in range(2):
            for qb, (_, tiles) in enumerate(geometry):
                for i, row_tiles in enumerate(tiles):
                    for jr, dr in enumerate(row_tiles):
                        tile = masked if dr is None else tiles_ref[head, dr]
                        bias_ref[head, qb, i * w:(i + 1) * w, jr * w:(jr + 1) * w] = tile

    ck = ck_ref[...].astype(BF16)
    cv = cv_ref[...].astype(BF16)
    for qb, (key0, _) in enumerate(geometry):
        q = q_ref[qb * nq:(qb + 1) * nq, :]
        k = k_ref[key0 * w:key0 * w + nk, :].astype(BF16)
        v = v_ref[key0 * w:key0 * w + nk, :].astype(BF16)
        outs = []
        for head in range(2):
            qh = jnp.where(_head_lanes(q.shape, head), q, 0.0).astype(BF16)
            s_loc = _dot_nt(qh, k) * scale + bias_ref[head, qb]
            s_ctx = _dot_nt(qh, ck) * scale
            m = jnp.maximum(jnp.max(s_loc, axis=-1, keepdims=True), jnp.max(s_ctx, axis=-1, keepdims=True))
            p_loc = jnp.exp(s_loc - m)
            p_ctx = jnp.exp(s_ctx - m)
            denom = jnp.sum(p_loc, axis=-1, keepdims=True) + jnp.sum(p_ctx, axis=-1, keepdims=True)
            outs.append((_dot(p_loc.astype(BF16), v) + _dot(p_ctx.astype(BF16), cv)) / denom)
        o_ref[qb * nq:(qb + 1) * nq, :] = jnp.where(_head_lanes(q.shape, 0), outs[0], outs[1])


def _nbr_attention(proj, row_block0, seq, n_batch, col0, ck, cv, tiles):
    pairs = B_HEADS // 2
    cb = col0 // LANES
    past = ck.shape[2]
    grid_rows = seq // GRID_W
    assert grid_rows % Q_ROWS == 0 and grid_rows >= KEY_ROWS
    geometry = _nbr_geometry(grid_rows)

    def col(section):
        return pl.BlockSpec((seq, LANES), lambda h, b: (row_block0 + b, cb + section * pairs + h))

    return pl.pallas_call(
        functools.partial(_nbr_attn_kernel, scale=1.0 / math.sqrt(LANES // 2), geometry=geometry),
        out_shape=jax.ShapeDtypeStruct((n_batch * seq, pairs * LANES), F32),
        grid=(pairs, n_batch),
        in_specs=[col(0), col(1), col(2),
                  pl.BlockSpec((None, None, past, LANES), lambda h, b: (b, h, 0, 0)),
                  pl.BlockSpec((None, None, past, LANES), lambda h, b: (b, h, 0, 0)),
                  pl.BlockSpec((2,) + tiles.shape[1:], lambda h, b: (h, 0, 0, 0))],
        out_specs=pl.BlockSpec((seq, LANES), lambda h, b: (b, h)),
        scratch_shapes=[pltpu.VMEM((2, len(geometry), Q_ROWS * GRID_W, KEY_ROWS * GRID_W), F32)],
        compiler_params=_params("arbitrary", "arbitrary"),
        name="neighbourhood_attention",
    )(proj, proj, proj, ck, cv, tiles)


def _head_pairs(t):
    b, h, l, dh = t.shape
    return t.reshape(b, h // 2, 2, l, dh).transpose(0, 1, 3, 2, 4).reshape(b, h // 2, l, 2 * dh)


def kernel(x_prompt, x_sample, c, state_hgrn, cache_k, cache_v, c_ctx, mod_w, mod_b, norm_pre, norm_post,
           ffn_w_in, ffn_w_out, ab_w_in, ab_w_out, a_lb, a_gnorm, b_rpb, cv_w_in, cv_conv, cv_w_out):
    n_ctx, ctx_len, d = x_prompt.shape
    n_lat, lat_len, _ = x_sample.shape
    depth = mod_w.shape[0]
    d_a = a_lb.shape[-1]
    rows = _Rows(n_ctx * ctx_len, ctx_len, n_lat * lat_len, lat_len, ROW_TILE)
    ffn_rows = _Rows(n_ctx * ctx_len, ctx_len, n_lat * lat_len, lat_len, FFN_ROW_TILE)
    assert (n_ctx * ctx_len) % lat_len == 0
    lat_block0 = (n_ctx * ctx_len) // lat_len

    x = (x_prompt.reshape(-1, d), x_sample.reshape(-1, d))

    cond = jnp.concatenate([c_ctx[None, :], c], axis=0)
    cond = jnp.pad(cond, ((0, -cond.shape[0] % 8), (0, 0)))
    mod = _modulation(cond, mod_w, mod_b)
    mod = mod.reshape(depth, cond.shape[0], 1, N_MOD * d)

    ffn_w_in = ffn_w_in.astype(BF16)
    ffn_w_out = ffn_w_out.astype(BF16)
    ab_w_in = ab_w_in.astype(BF16)
    ab_w_out = ab_w_out.astype(BF16)
    cv_w_in = cv_w_in.astype(BF16)
    cv_w_out = cv_w_out.astype(BF16)

    n_even = a_lb.shape[0]
    new_state, new_caches = None, None
    for l in range(depth):
        j = l // 2

        def gains(t, sub):
            return t[l, sub][None, :]

        x = _ffn(x, mod[l], gains(norm_pre, 0), gains(norm_post, 0), (ffn_w_in, (l, 0)), (ffn_w_out, (l, 0)),
                 ffn_rows, 0)
        if l % 2 == 0:
            proj = _inproj(x, mod[l], gains(norm_pre, 1), (ab_w_in, (j,)), rows)
            gn = a_gnorm[j][None, :]
            oa_ctx, new_state = _hgrn(proj, 0, ctx_len, n_ctx, a_lb, gn, j, emit_state=True,
                                      prior_state=new_state)
            oa_lat, = _hgrn(proj, lat_block0, lat_len, n_lat, a_lb, gn, j, s0=state_hgrn)
            ob_ctx, *new_caches = _ctx_attention(proj, n_ctx, ctx_len, 5 * d_a, n_even, j, new_caches)
            ob_lat = _nbr_attention(proj, lat_block0, lat_len, n_lat, 5 * d_a,
                                    _head_pairs(cache_k[:, j]), _head_pairs(cache_v[:, j]),
                                    _rpb_tiles(b_rpb[j]))
            x = _outproj(x, (oa_ctx, oa_lat), (ob_ctx, ob_lat), mod[l], gains(norm_post, 1), (ab_w_out, (j,)),
                         rows)
        else:
            x = _conv_mixer(x, mod[l], gains(norm_pre, 1), gains(norm_post, 1), (cv_w_in, (j,)), cv_conv[j],
                            (cv_w_out, (j,)), rows)
        x = _ffn(x, mod[l], gains(norm_pre, 2), gains(norm_post, 2), (ffn_w_in, (l, 1)), (ffn_w_out, (l, 1)),
                 ffn_rows, 2, split_out=(l == depth - 1))

    return (x[0].reshape(x_prompt.shape), x[1].reshape(x_sample.shape), new_state, *new_caches)
```

```python
import functools
import math

import jax
import jax.numpy as jnp
import numpy as np
from jax import lax
from jax.experimental import pallas as pl
from jax.experimental.pallas import tpu as pltpu

F32 = jnp.float32
BF16 = jnp.bfloat16

EPS = 1e-6
LB_MAX = 1.0 - 1e-4
MASKED = -1e30

GRID_W = 64
A_HEADS = 4
B_HEADS = 8
NA_ROWS = 8
NA_COLS = 16
CONV_W = 3
N_MOD = 9

VMEM_LIMIT_BYTES = 56 * 1024 * 1024
LANES = 128

ROW_TILE = 512
FFN_ROW_TILE = 1024
FFN_SUB_TILES = 4
HALO = 16
SCAN_CHUNK = 128
SCAN_GROUP = 16
SCAN_HEADS_PER_STEP = 2
SAFE_EXP2 = 115.0
LOG2_E = math.log2(math.e)
Q_ROWS = 4
KEY_ROWS = 12
SOFTMAX_ROWS = 16


def _params(*sem):
    return pltpu.CompilerParams(dimension_semantics=sem, vmem_limit_bytes=VMEM_LIMIT_BYTES)


def _rms(x, g):
    return x * lax.rsqrt(jnp.mean(x * x, axis=-1, keepdims=True) + EPS) * g


def _silu(x):
    return x * jax.nn.sigmoid(x)


def _dot(a, b):
    return jnp.dot(a, b, preferred_element_type=F32)


def _dot_nt(a, b):
    return lax.dot_general(a, b, (((1,), (1,)), ((), ())), preferred_element_type=F32)


def _dot_tn(a, b):
    return lax.dot_general(a, b, (((0,), (0,)), ((), ())), preferred_element_type=F32)


def _resident_spec(w, index):
    lead = len(index)
    zeros = (0,) * (w.ndim - lead)
    return pl.BlockSpec((None,) * lead + tuple(w.shape[lead:]), lambda *_: tuple(index) + zeros,
                        pipeline_mode=pl.Buffered(1))


def _mod_parts(mod, j, d):
    return (mod[:, (3 * j) * d:(3 * j + 1) * d],
            mod[:, (3 * j + 1) * d:(3 * j + 2) * d],
            mod[:, (3 * j + 2) * d:(3 * j + 3) * d])


class _Rows:
    def __init__(self, n_ctx_rows, ctx_len, n_lat_rows, lat_len, tile):
        assert ctx_len % tile == 0 or tile % ctx_len == 0
        assert lat_len % tile == 0 and n_ctx_rows % tile == 0 and n_lat_rows % tile == 0
        self.tile = tile
        self.n_ctx_tiles = n_ctx_rows // tile
        self.n_tiles = (n_ctx_rows + n_lat_rows) // tile
        self.lat_tiles_per_batch = lat_len // tile
        self.ctx_len = ctx_len
        self.lat_len = lat_len

    def mod_row(self, i):
        return jnp.where(i < self.n_ctx_tiles, 0, 1 + (i - self.n_ctx_tiles) // self.lat_tiles_per_batch)

    def group_specs(self, width):
        nc = self.n_ctx_tiles
        return [pl.BlockSpec((self.tile, width), lambda i: (jnp.minimum(i, nc - 1), 0)),
                pl.BlockSpec((self.tile, width), lambda i: (jnp.maximum(i - nc, 0), 0))]

    def mod_spec(self, width):
        return pl.BlockSpec((None, 1, width), lambda i: (self.mod_row(i), 0, 0))


def _mod_kernel(cond_ref, w_ref, b_ref, o_ref):
    s = _silu(cond_ref[...]).astype(BF16)
    o_ref[...] = _dot(s, w_ref[...].astype(BF16)) + b_ref[...]


def _modulation(cond, mod_w, mod_b):
    depth, d, n = mod_w.shape
    rows = cond.shape[0]
    tn = 1536
    assert n % tn == 0
    return pl.pallas_call(
        _mod_kernel,
        out_shape=jax.ShapeDtypeStruct((depth, rows, n), F32),
        grid=(depth, n // tn),
        in_specs=[pl.BlockSpec((rows, d), lambda l, j: (0, 0)),
                  pl.BlockSpec((None, d, tn), lambda l, j: (l, 0, j)),
                  pl.BlockSpec((None, 1, tn), lambda l, j: (l, 0, j))],
        out_specs=pl.BlockSpec((None, rows, tn), lambda l, j: (l, 0, j)),
        compiler_params=_params("parallel", "parallel"),
        name="adaln_mod",
    )(cond, mod_w, mod_b.reshape(depth, 1, n))


def _ffn_kernel(*refs, j, tf, n_sub, n_ctx_tiles, split_in, split_out):
    n_x = 2 if split_in else 1
    x_refs = refs[:n_x]
    mod_ref, gpre_ref, gpost_ref, win_ref, wout_ref = refs[n_x:n_x + 5]
    o_refs = refs[n_x + 5:-2]
    h_ref, act_ref = refs[-2:]
    is_ctx = pl.program_id(0) < n_ctx_tiles

    def load_x(rows):
        if split_in:
            return jnp.where(is_ctx, x_refs[0][rows, :], x_refs[1][rows, :])
        return x_refs[0][rows, :]

    tm, d = x_refs[0].shape
    dff = wout_ref.shape[0]
    shift, scale, gate = _mod_parts(mod_ref[...], j, d)
    for sub in range(n_sub):
        rows = slice(sub * (tm // n_sub), (sub + 1) * (tm // n_sub))
        h_ref[rows, :] = (_rms(load_x(rows), gpre_ref[...]) * (1.0 + scale) + shift).astype(BF16)
        for c in range(dff // tf):
            g = _dot(h_ref[rows, :], win_ref[:, c * tf:(c + 1) * tf])
            u = _dot(h_ref[rows, :], win_ref[:, dff + c * tf:dff + (c + 1) * tf])
            act_ref[rows, c * tf:(c + 1) * tf] = (_silu(g) * u).astype(BF16)
        out = _dot(act_ref[rows, :], wout_ref[...])
        y = load_x(rows) + (0.5 * gate) * _rms(out, gpost_ref[...])
        if split_out:
            @pl.when(is_ctx)
            def _():
                o_refs[0][rows, :] = y

            @pl.when(jnp.logical_not(is_ctx))
            def _():
                o_refs[1][rows, :] = y
        else:
            o_refs[0][rows, :] = y


def _ffn(x, mod_l, g_pre, g_post, w_in, w_out, rows, j, split_out=False):
    split_in = isinstance(x, tuple)
    d = x[0].shape[1] if split_in else x.shape[1]
    dff = w_out[0].shape[-2]
    tm = rows.tile
    tf = 256
    assert dff % tf == 0
    out_shape = jax.ShapeDtypeStruct((rows.n_tiles * tm, d), F32)
    out_spec = pl.BlockSpec((tm, d), lambda i: (i, 0))
    if split_out:
        out_shape = (jax.ShapeDtypeStruct((rows.n_ctx_tiles * tm, d), F32),
                     jax.ShapeDtypeStruct(((rows.n_tiles - rows.n_ctx_tiles) * tm, d), F32))
        out_spec = tuple(rows.group_specs(d))
    return pl.pallas_call(
        functools.partial(_ffn_kernel, j=j, tf=tf, n_sub=FFN_SUB_TILES, n_ctx_tiles=rows.n_ctx_tiles,
                          split_in=split_in, split_out=split_out),
        out_shape=out_shape,
        grid=(rows.n_tiles,),
        in_specs=(rows.group_specs(d) if split_in else [pl.BlockSpec((tm, d), lambda i: (i, 0))]) + [
                  rows.mod_spec(mod_l.shape[-1]),
                  pl.BlockSpec((1, d), lambda i: (0, 0)),
                  pl.BlockSpec((1, d), lambda i: (0, 0)),
                  _resident_spec(*w_in), _resident_spec(*w_out)],
        out_specs=out_spec,
        scratch_shapes=[pltpu.VMEM((tm, d), BF16), pltpu.VMEM((tm, dff), BF16)],
        compiler_params=_params("arbitrary" if split_out else "parallel"),
        name="swiglu_sublayer",
    )(*(x if split_in else (x,)), mod_l, g_pre, g_post, w_in[0], w_out[0])


def _inproj_kernel(x_ref, mod_ref, gpre_ref, w_ref, o_ref, h_ref, *, tn):
    d = x_ref.shape[1]
    shift, scale, _ = _mod_parts(mod_ref[...], 1, d)
    h_ref[...] = (_rms(x_ref[...], gpre_ref[...]) * (1.0 + scale) + shift).astype(BF16)
    for c in range(w_ref.shape[1] // tn):
        o_ref[:, c * tn:(c + 1) * tn] = _dot(h_ref[...], w_ref[:, c * tn:(c + 1) * tn])


def _inproj(x, mod_l, g_pre, w, rows):
    m, d = x.shape
    n = w[0].shape[-1]
    tm = rows.tile
    tn = 512
    assert n % tn == 0
    return pl.pallas_call(
        functools.partial(_inproj_kernel, tn=tn),
        out_shape=jax.ShapeDtypeStruct((m, n), F32),
        grid=(rows.n_tiles,),
        in_specs=[pl.BlockSpec((tm, d), lambda i: (i, 0)),
                  rows.mod_spec(mod_l.shape[-1]),
                  pl.BlockSpec((1, d), lambda i: (0, 0)),
                  _resident_spec(*w)],
        out_specs=pl.BlockSpec((tm, n), lambda i: (i, 0)),
        scratch_shapes=[pltpu.VMEM((tm, d), BF16)],
        compiler_params=_params("parallel"),
        name="mixer_in_proj",
    )(x, mod_l, g_pre, w[0])


def _outproj_kernel(x_ref, ac_ref, al_ref, bc_ref, bl_ref, mod_ref, gpost_ref, w_ref, o_ref, *, n_ctx_tiles):
    d = x_ref.shape[1]
    da = ac_ref.shape[1]
    is_ctx = pl.program_id(0) < n_ctx_tiles
    _, _, gate = _mod_parts(mod_ref[...], 1, d)
    a = jnp.where(is_ctx, ac_ref[...], al_ref[...]).astype(BF16)
    b = jnp.where(is_ctx, bc_ref[...], bl_ref[...]).astype(BF16)
    out = _dot(a, w_ref[:da, :]) + _dot(b, w_ref[da:, :])
    o_ref[...] = x_ref[...] + gate * _rms(out, gpost_ref[...])


def _outproj(x, o_a, o_b, mod_l, g_post, w, rows):
    m, d = x.shape
    tm = rows.tile
    return pl.pallas_call(
        functools.partial(_outproj_kernel, n_ctx_tiles=rows.n_ctx_tiles),
        out_shape=jax.ShapeDtypeStruct((m, d), F32),
        grid=(rows.n_tiles,),
        in_specs=[pl.BlockSpec((tm, d), lambda i: (i, 0))]
                 + rows.group_specs(o_a[0].shape[1]) + rows.group_specs(o_b[0].shape[1]) + [
                  rows.mod_spec(mod_l.shape[-1]),
                  pl.BlockSpec((1, d), lambda i: (0, 0)),
                  _resident_spec(*w)],
        out_specs=pl.BlockSpec((tm, d), lambda i: (i, 0)),
        compiler_params=_params("parallel"),
        name="mixer_out_proj",
    )(x, *o_a, *o_b, mod_l, g_post, w[0])


def _conv_kernel(x_ref, xp_ref, xn_ref, mod_ref, gpre_ref, gpost_ref, win_ref, cw_ref, wout_ref, o_ref,
                 h_ref, u_ref, z_ref, *, tc, rows):
    tm, d = x_ref.shape
    i = pl.program_id(0)
    shift, scale, gate = _mod_parts(mod_ref[...], 1, d)
    g_pre = gpre_ref[...]

    def pre(x):
        return (_rms(x, g_pre) * (1.0 + scale) + shift).astype(BF16)

    h_ref[0:HALO, :] = pre(xp_ref[...])
    h_ref[HALO:HALO + tm, :] = pre(x_ref[...])
    h_ref[HALO + tm:, :] = pre(xn_ref[...])

    seq_len = jnp.where(i < rows.n_ctx_tiles, rows.ctx_len, rows.lat_len)
    pos = (i * tm + lax.broadcasted_iota(jnp.int32, (tm, 1), 0)) & (seq_len - 1)
    has_left = pos != 0
    has_right = pos != seq_len - 1

    for c in range(d // tc):
        cols = slice(c * tc, (c + 1) * tc)
        b_gate = _dot(h_ref[HALO:HALO + tm, :], win_ref[:, c * tc:(c + 1) * tc])
        c_gate = _dot(h_ref[...], win_ref[:, d + c * tc:d + (c + 1) * tc])
        x_in = _dot(h_ref[...], win_ref[:, 2 * d + c * tc:2 * d + (c + 1) * tc])
        u_ref[...] = c_gate * x_in
        cw = cw_ref[:, cols]
        y = (cw[0:1, :] * jnp.where(has_left, u_ref[HALO - 1:HALO - 1 + tm, :], 0.0)
             + cw[1:2, :] * u_ref[HALO:HALO + tm, :]
             + cw[2:3, :] * jnp.where(has_right, u_ref[HALO + 1:HALO + 1 + tm, :], 0.0))
        z_ref[:, cols] = (b_gate * y).astype(BF16)
    out = _dot(z_ref[...], wout_ref[...])
    o_ref[...] = x_ref[...] + gate * _rms(out, gpost_ref[...])


def _conv_mixer(x, mod_l, g_pre, g_post, w_in, conv_w, w_out, rows):
    m, d = x.shape
    tm = rows.tile
    tc = 256
    assert rows.ctx_len & (rows.ctx_len - 1) == 0 and rows.lat_len & (rows.lat_len - 1) == 0
    hb = tm // HALO
    last = m // HALO - 1
    return pl.pallas_call(
        functools.partial(_conv_kernel, tc=tc, rows=rows),
        out_shape=jax.ShapeDtypeStruct((m, d), F32),
        grid=(rows.n_tiles,),
        in_specs=[pl.BlockSpec((tm, d), lambda i: (i, 0)),
                  pl.BlockSpec((HALO, d), lambda i: (jnp.maximum(i * hb - 1, 0), 0)),
                  pl.BlockSpec((HALO, d), lambda i: (jnp.minimum((i + 1) * hb, last), 0)),
                  rows.mod_spec(mod_l.shape[-1]),
                  pl.BlockSpec((1, d), lambda i: (0, 0)),
                  pl.BlockSpec((1, d), lambda i: (0, 0)),
                  _resident_spec(*w_in),
                  pl.BlockSpec((CONV_W, d), lambda i: (0, 0)),
                  _resident_spec(*w_out)],
        out_specs=pl.BlockSpec((tm, d), lambda i: (i, 0)),
        scratch_shapes=[pltpu.VMEM((tm + 2 * HALO, d), BF16),
                        pltpu.VMEM((tm + 2 * HALO, tc), F32),
                        pltpu.VMEM((tm, d), BF16)],
        compiler_params=_params("parallel"),
        name="conv_mixer",
    )(x, x, x, mod_l, g_pre, g_post, w_in[0], conv_w, w_out[0])


def _scan_constants(c):
    t, s = np.meshgrid(np.arange(c), np.arange(c), indexing="ij")
    x = t ^ s
    n_levels = int(math.log2(c // SCAN_GROUP))
    code = np.full((c, c), n_levels, np.int32)
    for lvl in range(n_levels - 1, 0, -1):
        code = np.where(x >= (c >> lvl), lvl, code)
    code = np.where(x < SCAN_GROUP, n_levels + 1, code)
    visited = np.stack([s <= t, s >= t])
    codes = np.where(visited, code[None], 0).astype(np.int32)
    return jnp.asarray(codes), jnp.asarray(visited, BF16), n_levels


def _gate_chunk(z, lb, cum, k_ref, c_ref, base):
    c_len, dk = z.shape
    e_abs = jnp.exp(-jnp.abs(z))
    inv = 1.0 / (1.0 + e_abs)
    sig = jnp.where(z >= 0, inv, e_abs * inv)
    sig_neg = jnp.where(z >= 0, e_abs * inv, inv)
    f = lb + (1.0 - lb) * sig
    log_sig = jnp.minimum(z, 0.0) + jnp.log(inv)
    log_f = jnp.maximum(jnp.log1p(-lb) + log_sig, jnp.log(jnp.maximum(f, 1e-30))) * LOG2_E
    k_ref[pl.ds(base, c_len), :] = (1.0 - lb) * sig_neg

    hi = log_f.astype(BF16)
    rest = log_f - hi.astype(F32)
    mid = rest.astype(BF16)
    lo = (rest - mid.astype(F32)).astype(BF16)
    sums = _dot(cum, jnp.concatenate([hi, mid, lo], axis=1))
    c = sums[:, :dk] + sums[:, dk:2 * dk] + sums[:, 2 * dk:]
    c_ref[pl.ds(base, c_len), :] = c

    span = jnp.zeros((1, dk), F32)
    for g in range(0, c_len, SCAN_GROUP):
        ends = c_ref[pl.ds(base + g, 1), :] - c_ref[pl.ds(base + g + SCAN_GROUP - 1, 1), :]
        span = jnp.maximum(span, jnp.abs(ends))
    return span


def _scan_chunk(q, k, c, v, c_row, st_ref, code, n_levels, forward, factorised):
    c_len, dk = q.shape
    q16 = q.astype(BF16)
    k16 = k.astype(BF16)

    def reference_rows(block, offset):
        parts = [jnp.broadcast_to(c_row(n * block + offset), (block, dk)) for n in range(c_len // block)]
        return parts[0] if len(parts) == 1 else jnp.concatenate(parts, axis=0)

    att = jnp.zeros((c_len, c_len), F32)
    for lvl in range(1, n_levels + 1):
        half = c_len >> lvl
        e = jnp.exp2(-jnp.abs(c - reference_rows(2 * half, half))).astype(BF16)
        att = jnp.where(code == lvl, _dot_nt(q16 * e, k16 * e), att)

    if factorised:
        dq = c - reference_rows(SCAN_GROUP, 0 if forward else SCAN_GROUP - 1)
        same = _dot_nt(q16 * jnp.exp2(dq).astype(BF16), k16 * jnp.exp2(-dq).astype(BF16))
    else:
        row = lax.broadcasted_iota(jnp.int32, (c_len, dk), 0)
        t = lax.broadcasted_iota(jnp.int32, (c_len, c_len), 0)
        s = lax.broadcasted_iota(jnp.int32, (c_len, c_len), 1)
        pos = row & (SCAN_GROUP - 1)
        same = jnp.zeros((c_len, c_len), F32)
        for dist in range(SCAN_GROUP):
            if dist == 0:
                k_s, c_s = k, c
            else:
                shift = dist if forward else c_len - dist
                k_s, c_s = pltpu.roll(k, shift, 0), pltpu.roll(c, shift, 0)
            valid = (pos >= dist) if forward else (pos + dist < SCAN_GROUP)
            term = q * k_s * jnp.exp2(jnp.where(valid, c - c_s, 0.0))
            a = jnp.sum(jnp.where(valid, term, 0.0), axis=-1, keepdims=True)
            same = jnp.where(s == (t - dist if forward else t + dist), a, same)
    att = jnp.where(code == n_levels + 1, same, att)

    st = st_ref[...]
    v16 = v.astype(BF16)
    o = _dot(att.astype(BF16), v16) + _dot(q16 * jnp.exp2(c).astype(BF16), st.astype(BF16))
    c_end = c_row(c_len - 1 if forward else 0)
    row_decay = jnp.broadcast_to(jnp.exp2(c_end), (dk, dk)).T
    st_ref[...] = st * row_decay + _dot_tn(k16 * jnp.exp2(c_end - c).astype(BF16), v16)
    return o


def _hgrn_kernel(*refs, layer_j, chunk, n_levels, heads, has_s0, n_prior, emit_state):
    q_ref, zf_ref, zb_ref, v_ref, g_ref, alb_ref, gn_ref, code_ref, sums_ref = refs[:9]
    n_in = 9 + int(has_s0) + (n_prior or 0)
    o_ref = refs[n_in]
    s_ref, o_dir_ref, k_ref, c_ref = refs[-4:]
    seq = q_ref.shape[0]
    dk = q_ref.shape[1] // heads
    n_chunks = seq // chunk
    z_refs = (zf_ref, zb_ref)

    def chunk_base(n):
        return pl.multiple_of(n * chunk, chunk)

    def lanes(head):
        return slice(head * dk, (head + 1) * dk)

    a = alb_ref[...]
    e = jnp.exp(a - jnp.max(a, axis=0, keepdims=True))
    p = e / jnp.sum(e, axis=0, keepdims=True)
    cum = p[0]
    for n in range(1, layer_j + 1):
        cum = cum + p[n]
    lb = jnp.clip(cum - p[0], 0.0, LB_MAX)

    def gates(n, span):
        base = chunk_base(n)
        rows = pl.ds(base, chunk)
        for head in range(heads):
            for direction in range(2):
                span = jnp.maximum(span, _gate_chunk(
                    z_refs[direction][rows, lanes(head)], lb[direction:direction + 1, lanes(head)],
                    sums_ref[direction], k_ref.at[direction, head], c_ref.at[direction, head], base))
        return span

    unroll = 2 if n_chunks % 2 == 0 else 1
    span = lax.fori_loop(0, n_chunks, gates, jnp.zeros((1, dk), F32), unroll=unroll)
    safe = jnp.max(span) <= SAFE_EXP2

    for head in range(heads):
        for direction in range(2):
            s_ref[direction, head] = refs[9][direction, head] if has_s0 else jnp.zeros((dk, dk), F32)

    def scan(factorised):
        def body(n, carry):
            for head in range(heads):
                for direction in range(2):
                    base = chunk_base(n if direction == 0 else n_chunks - 1 - n)
                    rows = pl.ds(base, chunk)
                    c_dir = c_ref.at[direction, head]
                    o_dir_ref[direction, head, rows, :] = _scan_chunk(
                        q_ref[rows, lanes(head)], k_ref[direction, head, rows, :], c_dir[rows, :],
                        v_ref[rows, lanes(head)], lambda r, c_dir=c_dir, base=base: c_dir[pl.ds(base + r, 1), :],
                        s_ref.at[direction, head], code_ref[direction], n_levels, direction == 0, factorised)
            return carry

        lax.fori_loop(0, n_chunks, body, 0, unroll=unroll if factorised else 1)

    lax.cond(safe, lambda: scan(True), lambda: scan(False))

    for head in range(heads):
        o = o_dir_ref[0, head] + o_dir_ref[1, head]
        o_ref[:, lanes(head)] = _rms(o, gn_ref[...]) * _silu(g_ref[:, lanes(head)])
        if emit_state:
            so_ref = refs[n_in + 1]
            if n_prior is not None:
                for n in range(n_prior):
                    so_ref[n, :, head] = refs[9 + int(has_s0) + n][:, head]
                so_ref = so_ref.at[n_prior]
            for direction in range(2):
                so_ref[direction, head] = s_ref[direction, head]


def _state_spec(layer_j, heads):
    return pl.BlockSpec((None, None, 2, heads, LANES, LANES), lambda b, h: (b, layer_j, 0, h, 0, 0))


def _hgrn(proj, row_block0, seq, n_batch, a_lb, gnorm, layer_j, s0=None, emit_state=False, prior=None):
    dk = LANES
    heads = SCAN_HEADS_PER_STEP
    groups = A_HEADS // heads
    chunk = min(SCAN_CHUNK, seq)
    codes, sums, n_levels = _scan_constants(chunk)
    n_layers = a_lb.shape[0]

    def col(section):
        return pl.BlockSpec((seq, heads * dk), lambda b, h: (row_block0 + b, section * groups + h))

    def const(a):
        return pl.BlockSpec(a.shape, lambda b, h: (0,) * a.ndim)

    layer_state_spec = pl.BlockSpec((None, 2, heads, dk, dk), lambda b, h: (b, 0, h, 0, 0))
    extra, extra_specs = [], []
    if s0 is not None:
        extra.append(s0)
        extra_specs.append(_state_spec(layer_j, heads))
    if prior is not None:
        extra += list(prior)
        extra_specs += [layer_state_spec] * len(prior)
    out_shape = [jax.ShapeDtypeStruct((n_batch * seq, A_HEADS * dk), F32)]
    out_specs = [pl.BlockSpec((seq, heads * dk), lambda b, h: (b, h))]
    if emit_state and prior is not None:
        stacked = len(prior) + 1
        out_shape.append(jax.ShapeDtypeStruct((n_batch, stacked, 2, A_HEADS, dk, dk), F32))
        out_specs.append(pl.BlockSpec((None, stacked, 2, heads, dk, dk), lambda b, h: (b, 0, 0, h, 0, 0)))
    elif emit_state:
        out_shape.append(jax.ShapeDtypeStruct((n_batch, 2, A_HEADS, dk, dk), F32))
        out_specs.append(layer_state_spec)
    per_head = (2, heads, seq, dk)
    return pl.pallas_call(
        functools.partial(_hgrn_kernel, layer_j=layer_j, chunk=chunk, n_levels=n_levels, heads=heads,
                          has_s0=s0 is not None, n_prior=None if prior is None else len(prior),
                          emit_state=emit_state),
        out_shape=tuple(out_shape),
        grid=(n_batch, groups),
        in_specs=[col(0), col(1), col(2), col(3), col(4),
                  pl.BlockSpec((n_layers, 2, heads * dk), lambda b, h: (0, 0, h)),
                  pl.BlockSpec((1, dk), lambda b, h: (0, 0)), const(codes), const(sums)] + extra_specs,
        out_specs=tuple(out_specs),
        scratch_shapes=[pltpu.VMEM((2, heads, dk, dk), F32), pltpu.VMEM(per_head, F32),
                        pltpu.VMEM(per_head, F32), pltpu.VMEM(per_head, F32)],
        compiler_params=_params("parallel", "parallel"),
        name="hgrn2_scan",
    )(proj, proj, proj, proj, proj, a_lb, gnorm, codes, sums, *extra)


def _head_lanes(shape, head):
    lane = lax.broadcasted_iota(jnp.int32, shape, len(shape) - 1)
    half = shape[-1] // 2
    return (lane < half) if head == 0 else (lane >= half)


def _ctx_attn_kernel(*refs, scale, n_prior):
    q_ref, k_ref, v_ref = refs[:3]
    o_ref, ko_ref, vo_ref = refs[-3:]
    dh = LANES // 2
    if n_prior is not None:
        for n in range(n_prior):
            ko_ref[n] = refs[3 + n][...]
            vo_ref[n] = refs[3 + n_prior + n][...]
        ko_ref, vo_ref = ko_ref.at[n_prior], vo_ref.at[n_prior]
    for pair in range(q_ref.shape[1] // LANES):
        lanes = slice(pair * LANES, (pair + 1) * LANES)
        k32 = k_ref[:, lanes]
        v32 = v_ref[:, lanes]
        k = k32.astype(BF16)
        v = v32.astype(BF16)
        q = q_ref[:, lanes]
        outs = []
        for head in range(2):
            ko_ref[2 * pair + head] = k32[:, head * dh:(head + 1) * dh]
            vo_ref[2 * pair + head] = v32[:, head * dh:(head + 1) * dh]
            qh = jnp.where(_head_lanes(q.shape, head), q, 0.0).astype(BF16)
            s = _dot_nt(qh, k) * scale
            p = jnp.exp(s - jnp.max(s, axis=-1, keepdims=True))
            outs.append(_dot(p.astype(BF16), v) / jnp.sum(p, axis=-1, keepdims=True))
        o_ref[:, lanes] = jnp.where(_head_lanes(q.shape, 0), outs[0], outs[1])


def _ctx_attention(proj, n_batch, seq, col0, prior, last):
    width = B_HEADS // 2 * LANES
    dh = LANES // 2
    cb = col0 // width
    assert col0 % width == 0
    layer_spec = pl.BlockSpec((None, B_HEADS, seq, dh), lambda b: (b, 0, 0, 0))
    if last:
        n_layers = len(prior) + 1
        cache = jax.ShapeDtypeStruct((n_batch, n_layers, B_HEADS, seq, dh), F32)
        cache_spec = pl.BlockSpec((None, n_layers, B_HEADS, seq, dh), lambda b: (b, 0, 0, 0, 0))
        earlier = [kv[0] for kv in prior] + [kv[1] for kv in prior]
    else:
        cache = jax.ShapeDtypeStruct((n_batch, B_HEADS, seq, dh), F32)
        cache_spec = layer_spec
        earlier = []

    def col(section):
        return pl.BlockSpec((seq, width), lambda b: (b, cb + section))

    return pl.pallas_call(
        functools.partial(_ctx_attn_kernel, scale=1.0 / math.sqrt(dh), n_prior=len(prior) if last else None),
        out_shape=(jax.ShapeDtypeStruct((n_batch * seq, width), F32), cache, cache),
        grid=(n_batch,),
        in_specs=[col(0), col(1), col(2)] + [layer_spec] * len(earlier),
        out_specs=(pl.BlockSpec((seq, width), lambda b: (b, 0)), cache_spec, cache_spec),
        compiler_params=_params("parallel"),
        name="context_attention",
    )(proj, proj, proj, *earlier)


def _rpb_tiles_kernel(rpb_ref, o_ref):
    h = pl.program_id(0)
    qc = lax.broadcasted_iota(jnp.int32, (GRID_W, GRID_W), 0)
    kc = lax.broadcasted_iota(jnp.int32, (GRID_W, GRID_W), 1)
    c0 = jnp.clip(qc - NA_COLS // 2, 0, GRID_W - NA_COLS)
    col_ok = (kc >= c0) & (kc < c0 + NA_COLS)
    dc = jnp.clip(kc - qc, -(NA_COLS - 1), NA_COLS - 1) + NA_COLS - 1
    for dr in range(2 * NA_ROWS - 1):
        t = jnp.zeros((GRID_W, GRID_W), F32)
        for i in range(2 * NA_COLS - 1):
            t = jnp.where(dc == i, rpb_ref[h, dr, i], t)
        o_ref[dr] = jnp.where(col_ok, t, MASKED)


def _rpb_tiles(rpb):
    heads, n_dr, n_dc = rpb.shape
    return pl.pallas_call(
        _rpb_tiles_kernel,
        out_shape=jax.ShapeDtypeStruct((heads, n_dr, GRID_W, GRID_W), F32),
        grid=(heads,),
        in_specs=[pl.BlockSpec(memory_space=pltpu.SMEM)],
        out_specs=pl.BlockSpec((None, n_dr, GRID_W, GRID_W), lambda h: (h, 0, 0, 0)),
        compiler_params=_params("parallel"),
        name="rpb_tiles",
    )(rpb)


def _nbr_geometry(grid_rows):
    win = min(NA_ROWS, grid_rows)
    blocks = []
    for qb in range(grid_rows // Q_ROWS):
        r0 = qb * Q_ROWS
        key0 = min(max(r0 - win // 2, 0), grid_rows - KEY_ROWS)
        tiles = []
        for i in range(Q_ROWS):
            r = r0 + i
            start = min(max(r - win // 2, 0), grid_rows - win)
            assert key0 <= start and start + win <= key0 + KEY_ROWS
            tiles.append([(key0 + jr - r + NA_ROWS - 1) if start <= key0 + jr < start + win else None
                          for jr in range(KEY_ROWS)])
        blocks.append((key0, tiles))
    return blocks


def _nbr_attn_kernel(q_ref, k_ref, v_ref, ck_ref, cv_ref, tiles_ref, o_ref, bias_ref, kk_ref, vv_ref, s_ref,
                     p_ref, inv_ref, *, scale, geometry):
    w = GRID_W
    nq = Q_ROWS * w
    nk = KEY_ROWS * w

    @pl.when(pl.program_id(1) == 0)
    def _():
        masked = jnp.full((w, w), MASKED, F32)
        for head in range(2):
            for qb, (_, tiles) in enumerate(geometry):
                for i, row_tiles in enumerate(tiles):
                    for jr, dr in enumerate(row_tiles):
                        tile = masked if dr is None else tiles_ref[head, dr]
                        bias_ref[head, qb, i * w:(i + 1) * w, jr * w:(jr + 1) * w] = tile

    for slot in range(2):
        kk_ref[slot, nk:, :] = ck_ref[...].astype(BF16)
        vv_ref[slot, nk:, :] = cv_ref[...].astype(BF16)
    iteration = 0
    for qb, (key0, _) in enumerate(geometry):
        q = q_ref[qb * nq:(qb + 1) * nq, :] * scale
        kk, vv = kk_ref.at[qb % 2], vv_ref.at[qb % 2]
        kk[:nk, :] = k_ref[key0 * w:key0 * w + nk, :].astype(BF16)
        vv[:nk, :] = v_ref[key0 * w:key0 * w + nk, :].astype(BF16)
        outs = []
        for head in range(2):
            s, p, inv = (r.at[iteration % 2] for r in (s_ref, p_ref, inv_ref))
            iteration += 1
            qh = jnp.where(_head_lanes(q.shape, head), q, 0.0).astype(BF16)
            s[...] = _dot_nt(qh, kk[...])
            for r in range(nq // SOFTMAX_ROWS):
                rows = slice(r * SOFTMAX_ROWS, (r + 1) * SOFTMAX_ROWS)
                s_loc = s[rows, :nk] + bias_ref[head, qb, rows, :]
                s_ctx = s[rows, nk:]
                m = jnp.maximum(jnp.max(s_loc, axis=-1, keepdims=True), jnp.max(s_ctx, axis=-1, keepdims=True))
                p_loc = jnp.exp(s_loc - m)
                p_ctx = jnp.exp(s_ctx - m)
                denom = jnp.sum(p_loc, axis=-1, keepdims=True) + jnp.sum(p_ctx, axis=-1, keepdims=True)
                p[rows, :nk] = p_loc.astype(BF16)
                p[rows, nk:] = p_ctx.astype(BF16)
                inv[rows, :] = jnp.broadcast_to(1.0 / denom, (SOFTMAX_ROWS, inv.shape[1]))
            outs.append(_dot(p[...], vv[...]) * inv[...])
        o_ref[qb * nq:(qb + 1) * nq, :] = jnp.where(_head_lanes(q.shape, 0), outs[0], outs[1])


def _nbr_attention(proj, row_block0, seq, n_batch, col0, ck, cv, tiles):
    pairs = B_HEADS // 2
    cb = col0 // LANES
    past = ck.shape[2]
    grid_rows = seq // GRID_W
    assert grid_rows % Q_ROWS == 0 and grid_rows >= KEY_ROWS
    geometry = _nbr_geometry(grid_rows)
    nq, nk = Q_ROWS * GRID_W, KEY_ROWS * GRID_W

    def col(section):
        return pl.BlockSpec((seq, LANES), lambda h, b: (row_block0 + b, cb + section * pairs + h))

    return pl.pallas_call(
        functools.partial(_nbr_attn_kernel, scale=1.0 / math.sqrt(LANES // 2), geometry=geometry),
        out_shape=jax.ShapeDtypeStruct((n_batch * seq, pairs * LANES), F32),
        grid=(pairs, n_batch),
        in_specs=[col(0), col(1), col(2),
                  pl.BlockSpec((None, None, past, LANES), lambda h, b: (b, h, 0, 0)),
                  pl.BlockSpec((None, None, past, LANES), lambda h, b: (b, h, 0, 0)),
                  pl.BlockSpec((2,) + tiles.shape[1:], lambda h, b: (h, 0, 0, 0))],
        out_specs=pl.BlockSpec((seq, LANES), lambda h, b: (b, h)),
        scratch_shapes=[pltpu.VMEM((2, len(geometry), nq, nk), F32),
                        pltpu.VMEM((2, nk + past, LANES), BF16), pltpu.VMEM((2, nk + past, LANES), BF16),
                        pltpu.VMEM((2, nq, nk + past), F32), pltpu.VMEM((2, nq, nk + past), BF16),
                        pltpu.VMEM((2, nq, LANES), F32)],
        compiler_params=_params("arbitrary", "arbitrary"),
        name="neighbourhood_attention",
    )(proj, proj, proj, ck, cv, tiles)


def _head_pairs(t):
    b, h, l, dh = t.shape
    return t.reshape(b, h // 2, 2, l, dh).transpose(0, 1, 3, 2, 4).reshape(b, h // 2, l, 2 * dh)


def kernel(x_prompt, x_sample, c, state_hgrn, cache_k, cache_v, c_ctx, mod_w, mod_b, norm_pre, norm_post,
           ffn_w_in, ffn_w_out, ab_w_in, ab_w_out, a_lb, a_gnorm, b_rpb, cv_w_in, cv_conv, cv_w_out):
    n_ctx, ctx_len, d = x_prompt.shape
    n_lat, lat_len, _ = x_sample.shape
    depth = mod_w.shape[0]
    d_a = a_lb.shape[-1]
    rows = _Rows(n_ctx * ctx_len, ctx_len, n_lat * lat_len, lat_len, ROW_TILE)
    ffn_rows = _Rows(n_ctx * ctx_len, ctx_len, n_lat * lat_len, lat_len, FFN_ROW_TILE)
    assert (n_ctx * ctx_len) % lat_len == 0
    lat_block0 = (n_ctx * ctx_len) // lat_len

    x = (x_prompt.reshape(-1, d), x_sample.reshape(-1, d))

    cond = jnp.concatenate([c_ctx[None, :], c], axis=0)
    cond = jnp.pad(cond, ((0, -cond.shape[0] % 8), (0, 0)))
    mod = _modulation(cond, mod_w, mod_b)
    mod = mod.reshape(depth, cond.shape[0], 1, N_MOD * d)

    ffn_w_in = ffn_w_in.astype(BF16)
    ffn_w_out = ffn_w_out.astype(BF16)
    ab_w_in = ab_w_in.astype(BF16)
    ab_w_out = ab_w_out.astype(BF16)
    cv_w_in = cv_w_in.astype(BF16)
    cv_w_out = cv_w_out.astype(BF16)

    n_even = a_lb.shape[0]
    layer_states, layer_caches = [], []
    for l in range(depth):
        j = l // 2

        def gains(t, sub):
            return t[l, sub][None, :]

        x = _ffn(x, mod[l], gains(norm_pre, 0), gains(norm_post, 0), (ffn_w_in, (l, 0)), (ffn_w_out, (l, 0)),
                 ffn_rows, 0)
        if l % 2 == 0:
            proj = _inproj(x, mod[l], gains(norm_pre, 1), (ab_w_in, (j,)), rows)
            gn = a_gnorm[j][None, :]
            last = j == n_even - 1
            oa_ctx, new_state = _hgrn(proj, 0, ctx_len, n_ctx, a_lb, gn, j, emit_state=True,
                                      prior=layer_states if last else None)
            oa_lat, = _hgrn(proj, lat_block0, lat_len, n_lat, a_lb, gn, j, s0=state_hgrn)
            ob_ctx, *new_caches = _ctx_attention(proj, n_ctx, ctx_len, 5 * d_a, layer_caches, last)
            layer_states.append(new_state)
            layer_caches.append(new_caches)
            ob_lat = _nbr_attention(proj, lat_block0, lat_len, n_lat, 5 * d_a,
                                    _head_pairs(cache_k[:, j]), _head_pairs(cache_v[:, j]),
                                    _rpb_tiles(b_rpb[j]))
            x = _outproj(x, (oa_ctx, oa_lat), (ob_ctx, ob_lat), mod[l], gains(norm_post, 1), (ab_w_out, (j,)),
                         rows)
        else:
            x = _conv_mixer(x, mod[l], gains(norm_pre, 1), gains(norm_post, 1), (cv_w_in, (j,)), cv_conv[j],
                            (cv_w_out, (j,)), rows)
        x = _ffn(x, mod[l], gains(norm_pre, 2), gains(norm_post, 2), (ffn_w_in, (l, 1)), (ffn_w_out, (l, 1)),
                 ffn_rows, 2, split_out=(l == depth - 1))

    return (x[0].reshape(x_prompt.shape), x[1].reshape(x_sample.shape), new_state, *new_caches)
```

```python
import functools
import math

import jax
import jax.numpy as jnp
import numpy as np
from jax import lax
from jax.experimental import pallas as pl
from jax.experimental.pallas import tpu as pltpu

F32 = jnp.float32
BF16 = jnp.bfloat16

EPS = 1e-6
LB_MAX = 1.0 - 1e-4
MASKED = -1e30

GRID_W = 64
A_HEADS = 4
B_HEADS = 8
NA_ROWS = 8
NA_COLS = 16
CONV_W = 3
N_MOD = 9

VMEM_LIMIT_BYTES = 56 * 1024 * 1024
LANES = 128

ROW_TILE = 512
FFN_ROW_TILE = 512
FFN_SUB_TILES = 2
FUSED_FFN_ROW_TILE = 512
HALO = 16
SCAN_CHUNK = 128
SCAN_GROUP = 16
SCAN_HEADS_PER_STEP = 2
SAFE_EXP2 = 115.0
LOG2_E = math.log2(math.e)
Q_ROWS = 4
KEY_ROWS = 12


def _params(*sem):
    return pltpu.CompilerParams(dimension_semantics=sem, vmem_limit_bytes=VMEM_LIMIT_BYTES)


def _rms(x, g):
    return x * lax.rsqrt(jnp.mean(x * x, axis=-1, keepdims=True) + EPS) * g


def _silu(x):
    return x * jax.nn.sigmoid(x)


def _dot(a, b):
    return jnp.dot(a, b, preferred_element_type=F32)


def _dot_nt(a, b):
    return lax.dot_general(a, b, (((1,), (1,)), ((), ())), preferred_element_type=F32)


def _dot_tn(a, b):
    return lax.dot_general(a, b, (((0,), (0,)), ((), ())), preferred_element_type=F32)


def _resident_spec(w, index):
    lead = len(index)
    zeros = (0,) * (w.ndim - lead)
    return pl.BlockSpec((None,) * lead + tuple(w.shape[lead:]), lambda *_: tuple(index) + zeros,
                        pipeline_mode=pl.Buffered(1))


def _mod_parts(mod, j, d):
    return (mod[:, (3 * j) * d:(3 * j + 1) * d],
            mod[:, (3 * j + 1) * d:(3 * j + 2) * d],
            mod[:, (3 * j + 2) * d:(3 * j + 3) * d])


class _Rows:
    def __init__(self, n_ctx_rows, ctx_len, n_lat_rows, lat_len, tile):
        assert ctx_len % tile == 0 or tile % ctx_len == 0
        assert lat_len % tile == 0 and n_ctx_rows % tile == 0 and n_lat_rows % tile == 0
        self.tile = tile
        self.n_ctx_tiles = n_ctx_rows // tile
        self.n_tiles = (n_ctx_rows + n_lat_rows) // tile
        self.lat_tiles_per_batch = lat_len // tile
        self.ctx_len = ctx_len
        self.lat_len = lat_len

    def mod_row(self, i):
        return jnp.where(i < self.n_ctx_tiles, 0, 1 + (i - self.n_ctx_tiles) // self.lat_tiles_per_batch)

    def group_specs(self, width):
        nc = self.n_ctx_tiles
        return [pl.BlockSpec((self.tile, width), lambda i: (jnp.minimum(i, nc - 1), 0)),
                pl.BlockSpec((self.tile, width), lambda i: (jnp.maximum(i - nc, 0), 0))]

    def mod_spec(self, width):
        return pl.BlockSpec((None, 1, width), lambda i: (self.mod_row(i), 0, 0))


def _mod_kernel(cond_ref, w_ref, b_ref, o_ref):
    s = _silu(cond_ref[...]).astype(BF16)
    o_ref[...] = _dot(s, w_ref[...].astype(BF16)) + b_ref[...]


def _modulation(cond, mod_w, mod_b):
    depth, d, n = mod_w.shape
    rows = cond.shape[0]
    tn = 1536
    assert n % tn == 0
    return pl.pallas_call(
        _mod_kernel,
        out_shape=jax.ShapeDtypeStruct((depth, rows, n), F32),
        grid=(depth, n // tn),
        in_specs=[pl.BlockSpec((rows, d), lambda l, j: (0, 0)),
                  pl.BlockSpec((None, d, tn), lambda l, j: (l, 0, j)),
                  pl.BlockSpec((None, 1, tn), lambda l, j: (l, 0, j))],
        out_specs=pl.BlockSpec((None, rows, tn), lambda l, j: (l, 0, j)),
        compiler_params=_params("parallel", "parallel"),
        name="adaln_mod",
    )(cond, mod_w, mod_b.reshape(depth, 1, n))


def _ffn_kernel(*refs, j, tf, n_sub, n_ctx_tiles, split_in, split_out, fuse_mixer):
    n_x = 2 if split_in else 1
    x_refs = refs[:n_x]
    n_mix = 6 if fuse_mixer else 0
    mix_refs = refs[n_x:n_x + n_mix]
    mod_ref, gpre_ref, gpost_ref, win_ref, wout_ref = refs[n_x + n_mix:n_x + n_mix + 5]
    n_scratch = 3 if fuse_mixer else 2
    o_refs = refs[n_x + n_mix + 5:-n_scratch]
    h_ref, act_ref = refs[-n_scratch:][:2]
    is_ctx = pl.program_id(0) < n_ctx_tiles

    def load_x(rows):
        if split_in:
            return jnp.where(is_ctx, x_refs[0][rows, :], x_refs[1][rows, :])
        return x_refs[0][rows, :]

    tm, d = x_refs[0].shape
    dff = wout_ref.shape[0]
    shift, scale, gate = _mod_parts(mod_ref[...], j, d)
    sub_rows = [slice(sub * (tm // n_sub), (sub + 1) * (tm // n_sub)) for sub in range(n_sub)]
    for rows in sub_rows:
        x_in = load_x(rows)
        if fuse_mixer:
            ac_ref, al_ref, bc_ref, bl_ref, wmix_ref, gmix_ref = mix_refs
            da = ac_ref.shape[1]
            a = jnp.where(is_ctx, ac_ref[rows, :], al_ref[rows, :]).astype(BF16)
            b = jnp.where(is_ctx, bc_ref[rows, :], bl_ref[rows, :]).astype(BF16)
            mixed = _dot(a, wmix_ref[:da, :]) + _dot(b, wmix_ref[da:, :])
            x_in = x_in + _mod_parts(mod_ref[...], 1, d)[2] * _rms(mixed, gmix_ref[...])
            refs[-1][rows, :] = x_in
        h_ref[rows, :] = (_rms(x_in, gpre_ref[...]) * (1.0 + scale) + shift).astype(BF16)
    for c in range(dff // tf):
        w_gate = win_ref[:, c * tf:(c + 1) * tf].astype(BF16)
        w_up = win_ref[:, dff + c * tf:dff + (c + 1) * tf].astype(BF16)
        for rows in sub_rows:
            g = _dot(h_ref[rows, :], w_gate)
            u = _dot(h_ref[rows, :], w_up)
            act_ref[rows, c * tf:(c + 1) * tf] = (_silu(g) * u).astype(BF16)
    for rows in sub_rows:
        out = _dot(act_ref[rows, :], wout_ref[...])
        x_in = refs[-1][rows, :] if fuse_mixer else load_x(rows)
        y = x_in + (0.5 * gate) * _rms(out, gpost_ref[...])
        if split_out:
            @pl.when(is_ctx)
            def _():
                o_refs[0][rows, :] = y

            @pl.when(jnp.logical_not(is_ctx))
            def _():
                o_refs[1][rows, :] = y
        else:
            o_refs[0][rows, :] = y


def _ffn(x, mod_l, g_pre, g_post, w_in, w_out, rows, j, split_out=False, mixer=None):
    split_in = isinstance(x, tuple)
    n_sub = FFN_SUB_TILES if rows.tile >= FFN_ROW_TILE else 1
    mix_args, mix_specs = [], []
    if mixer is not None:
        o_a, o_b, w_mix, g_mix = mixer
        mix_args = [*o_a, *o_b, w_mix[0], g_mix]
        mix_specs = (rows.group_specs(o_a[0].shape[1]) + rows.group_specs(o_b[0].shape[1])
                     + [_resident_spec(*w_mix), pl.BlockSpec((1, g_mix.shape[1]), lambda i: (0, 0))])
    d = x[0].shape[1] if split_in else x.shape[1]
    dff = w_out[0].shape[-2]
    tm = rows.tile
    tf = 256
    assert dff % tf == 0
    out_shape = jax.ShapeDtypeStruct((rows.n_tiles * tm, d), F32)
    out_spec = pl.BlockSpec((tm, d), lambda i: (i, 0))
    if split_out:
        out_shape = (jax.ShapeDtypeStruct((rows.n_ctx_tiles * tm, d), F32),
                     jax.ShapeDtypeStruct(((rows.n_tiles - rows.n_ctx_tiles) * tm, d), F32))
        out_spec = tuple(rows.group_specs(d))
    return pl.pallas_call(
        functools.partial(_ffn_kernel, j=j, tf=tf, n_sub=n_sub, n_ctx_tiles=rows.n_ctx_tiles,
                          split_in=split_in, split_out=split_out, fuse_mixer=mixer is not None),
        out_shape=out_shape,
        grid=(rows.n_tiles,),
        in_specs=(rows.group_specs(d) if split_in else [pl.BlockSpec((tm, d), lambda i: (i, 0))]) + mix_specs + [
                  rows.mod_spec(mod_l.shape[-1]),
                  pl.BlockSpec((1, d), lambda i: (0, 0)),
                  pl.BlockSpec((1, d), lambda i: (0, 0)),
                  _resident_spec(*w_in), _resident_spec(*w_out)],
        out_specs=out_spec,
        scratch_shapes=[pltpu.VMEM((tm, d), BF16), pltpu.VMEM((tm, dff), BF16)]
                       + ([pltpu.VMEM((tm, d), F32)] if mixer is not None else []),
        compiler_params=_params("arbitrary" if split_out else "parallel"),
        name="swiglu_sublayer",
    )(*(x if split_in else (x,)), *mix_args, mod_l, g_pre, g_post, w_in[0], w_out[0])


def _inproj_kernel(x_ref, mod_ref, gpre_ref, w_ref, o_ref, h_ref, *, tn):
    d = x_ref.shape[1]
    shift, scale, _ = _mod_parts(mod_ref[...], 1, d)
    h_ref[...] = (_rms(x_ref[...], gpre_ref[...]) * (1.0 + scale) + shift).astype(BF16)
    for c in range(w_ref.shape[1] // tn):
        o_ref[:, c * tn:(c + 1) * tn] = _dot(h_ref[...], w_ref[:, c * tn:(c + 1) * tn])


def _inproj(x, mod_l, g_pre, w, rows):
    m, d = x.shape
    n = w[0].shape[-1]
    tm = rows.tile
    tn = 512
    assert n % tn == 0
    return pl.pallas_call(
        functools.partial(_inproj_kernel, tn=tn),
        out_shape=jax.ShapeDtypeStruct((m, n), F32),
        grid=(rows.n_tiles,),
        in_specs=[pl.BlockSpec((tm, d), lambda i: (i, 0)),
                  rows.mod_spec(mod_l.shape[-1]),
                  pl.BlockSpec((1, d), lambda i: (0, 0)),
                  _resident_spec(*w)],
        out_specs=pl.BlockSpec((tm, n), lambda i: (i, 0)),
        scratch_shapes=[pltpu.VMEM((tm, d), BF16)],
        compiler_params=_params("parallel"),
        name="mixer_in_proj",
    )(x, mod_l, g_pre, w[0])


def _conv_kernel(x_ref, xp_ref, xn_ref, mod_ref, gpre_ref, gpost_ref, win_ref, cw_ref, wout_ref, o_ref,
                 h_ref, u_ref, z_ref, *, tc, rows):
    tm, d = x_ref.shape
    i = pl.program_id(0)
    shift, scale, gate = _mod_parts(mod_ref[...], 1, d)
    g_pre = gpre_ref[...]

    def pre(x):
        return (_rms(x, g_pre) * (1.0 + scale) + shift).astype(BF16)

    h_ref[0:HALO, :] = pre(xp_ref[...])
    h_ref[HALO:HALO + tm, :] = pre(x_ref[...])
    h_ref[HALO + tm:, :] = pre(xn_ref[...])

    seq_len = jnp.where(i < rows.n_ctx_tiles, rows.ctx_len, rows.lat_len)
    pos = (i * tm + lax.broadcasted_iota(jnp.int32, (tm, 1), 0)) & (seq_len - 1)
    has_left = pos != 0
    has_right = pos != seq_len - 1

    for c in range(d // tc):
        cols = slice(c * tc, (c + 1) * tc)
        b_gate = _dot(h_ref[HALO:HALO + tm, :], win_ref[:, c * tc:(c + 1) * tc])
        c_gate = _dot(h_ref[...], win_ref[:, d + c * tc:d + (c + 1) * tc])
        x_in = _dot(h_ref[...], win_ref[:, 2 * d + c * tc:2 * d + (c + 1) * tc])
        u_ref[...] = c_gate * x_in
        cw = cw_ref[:, cols]
        y = (cw[0:1, :] * jnp.where(has_left, u_ref[HALO - 1:HALO - 1 + tm, :], 0.0)
             + cw[1:2, :] * u_ref[HALO:HALO + tm, :]
             + cw[2:3, :] * jnp.where(has_right, u_ref[HALO + 1:HALO + 1 + tm, :], 0.0))
        z_ref[:, cols] = (b_gate * y).astype(BF16)
    out = _dot(z_ref[...], wout_ref[...])
    o_ref[...] = x_ref[...] + gate * _rms(out, gpost_ref[...])


def _conv_mixer(x, mod_l, g_pre, g_post, w_in, conv_w, w_out, rows):
    m, d = x.shape
    tm = rows.tile
    tc = 256
    assert rows.ctx_len & (rows.ctx_len - 1) == 0 and rows.lat_len & (rows.lat_len - 1) == 0
    hb = tm // HALO
    last = m // HALO - 1
    return pl.pallas_call(
        functools.partial(_conv_kernel, tc=tc, rows=rows),
        out_shape=jax.ShapeDtypeStruct((m, d), F32),
        grid=(rows.n_tiles,),
        in_specs=[pl.BlockSpec((tm, d), lambda i: (i, 0)),
                  pl.BlockSpec((HALO, d), lambda i: (jnp.maximum(i * hb - 1, 0), 0)),
                  pl.BlockSpec((HALO, d), lambda i: (jnp.minimum((i + 1) * hb, last), 0)),
                  rows.mod_spec(mod_l.shape[-1]),
                  pl.BlockSpec((1, d), lambda i: (0, 0)),
                  pl.BlockSpec((1, d), lambda i: (0, 0)),
                  _resident_spec(*w_in),
                  pl.BlockSpec((CONV_W, d), lambda i: (0, 0)),
                  _resident_spec(*w_out)],
        out_specs=pl.BlockSpec((tm, d), lambda i: (i, 0)),
        scratch_shapes=[pltpu.VMEM((tm + 2 * HALO, d), BF16),
                        pltpu.VMEM((tm + 2 * HALO, tc), F32),
                        pltpu.VMEM((tm, d), BF16)],
        compiler_params=_params("parallel"),
        name="conv_mixer",
    )(x, x, x, mod_l, g_pre, g_post, w_in[0], conv_w, w_out[0])


def _scan_constants(c):
    t, s = np.meshgrid(np.arange(c), np.arange(c), indexing="ij")
    x = t ^ s
    n_levels = int(math.log2(c // SCAN_GROUP))
    code = np.full((c, c), n_levels, np.int32)
    for lvl in range(n_levels - 1, 0, -1):
        code = np.where(x >= (c >> lvl), lvl, code)
    code = np.where(x < SCAN_GROUP, n_levels + 1, code)
    visited = np.stack([s <= t, s >= t])
    codes = np.where(visited, code[None], 0).astype(np.int32)
    return jnp.asarray(codes), jnp.asarray(visited, BF16), n_levels


def _gate_chunk(z, lb, cum, k_ref, c_ref, base):
    c_len, dk = z.shape
    e_abs = jnp.exp(-jnp.abs(z))
    inv = 1.0 / (1.0 + e_abs)
    sig = jnp.where(z >= 0, inv, e_abs * inv)
    sig_neg = jnp.where(z >= 0, e_abs * inv, inv)
    f = lb + (1.0 - lb) * sig
    log_sig = jnp.minimum(z, 0.0) + jnp.log(inv)
    log_f = jnp.maximum(jnp.log1p(-lb) + log_sig, jnp.log(jnp.maximum(f, 1e-30))) * LOG2_E
    k_ref[pl.ds(base, c_len), :] = (1.0 - lb) * sig_neg

    hi = log_f.astype(BF16)
    rest = log_f - hi.astype(F32)
    mid = rest.astype(BF16)
    lo = (rest - mid.astype(F32)).astype(BF16)
    sums = _dot(cum, jnp.concatenate([hi, mid, lo], axis=1))
    c = sums[:, :dk] + sums[:, dk:2 * dk] + sums[:, 2 * dk:]
    c_ref[pl.ds(base, c_len), :] = c

    span = jnp.zeros((1, dk), F32)
    for g in range(0, c_len, SCAN_GROUP):
        ends = c_ref[pl.ds(base + g, 1), :] - c_ref[pl.ds(base + g + SCAN_GROUP - 1, 1), :]
        span = jnp.maximum(span, jnp.abs(ends))
    return span


def _scan_chunk(q, k, c, v, c_row, st_ref, code, n_levels, forward, factorised):
    c_len, dk = q.shape
    q16 = q.astype(BF16)
    k16 = k.astype(BF16)

    def reference_rows(block, offset):
        parts = [jnp.broadcast_to(c_row(n * block + offset), (block, dk)) for n in range(c_len // block)]
        return parts[0] if len(parts) == 1 else jnp.concatenate(parts, axis=0)

    att = jnp.zeros((c_len, c_len), F32)
    for lvl in range(1, n_levels + 1):
        half = c_len >> lvl
        e = jnp.exp2(-jnp.abs(c - reference_rows(2 * half, half))).astype(BF16)
        att = jnp.where(code == lvl, _dot_nt(q16 * e, k16 * e), att)

    if factorised:
        dq = c - reference_rows(SCAN_GROUP, 0 if forward else SCAN_GROUP - 1)
        same = _dot_nt(q16 * jnp.exp2(dq).astype(BF16), k16 * jnp.exp2(-dq).astype(BF16))
    else:
        row = lax.broadcasted_iota(jnp.int32, (c_len, dk), 0)
        t = lax.broadcasted_iota(jnp.int32, (c_len, c_len), 0)
        s = lax.broadcasted_iota(jnp.int32, (c_len, c_len), 1)
        pos = row & (SCAN_GROUP - 1)
        same = jnp.zeros((c_len, c_len), F32)
        for dist in range(SCAN_GROUP):
            if dist == 0:
                k_s, c_s = k, c
            else:
                shift = dist if forward else c_len - dist
                k_s, c_s = pltpu.roll(k, shift, 0), pltpu.roll(c, shift, 0)
            valid = (pos >= dist) if forward else (pos + dist < SCAN_GROUP)
            term = q * k_s * jnp.exp2(jnp.where(valid, c - c_s, 0.0))
            a = jnp.sum(jnp.where(valid, term, 0.0), axis=-1, keepdims=True)
            same = jnp.where(s == (t - dist if forward else t + dist), a, same)
    att = jnp.where(code == n_levels + 1, same, att)

    st = st_ref[...]
    v16 = v.astype(BF16)
    o = _dot(att.astype(BF16), v16) + _dot(q16 * jnp.exp2(c).astype(BF16), st.astype(BF16))
    c_end = c_row(c_len - 1 if forward else 0)
    row_decay = jnp.broadcast_to(jnp.exp2(c_end), (dk, dk)).T
    st_ref[...] = st * row_decay + _dot_tn(k16 * jnp.exp2(c_end - c).astype(BF16), v16)
    return o


def _hgrn_kernel(*refs, layer_j, chunk, n_levels, heads, has_s0, n_prior, emit_state):
    q_ref, zf_ref, zb_ref, v_ref, g_ref, alb_ref, gn_ref, code_ref, sums_ref = refs[:9]
    n_in = 9 + int(has_s0) + (n_prior or 0)
    o_ref = refs[n_in]
    s_ref, o_dir_ref, k_ref, c_ref = refs[-4:]
    seq = q_ref.shape[0]
    dk = q_ref.shape[1] // heads
    n_chunks = seq // chunk
    z_refs = (zf_ref, zb_ref)

    def chunk_base(n):
        return pl.multiple_of(n * chunk, chunk)

    def lanes(head):
        return slice(head * dk, (head + 1) * dk)

    a = alb_ref[...]
    e = jnp.exp(a - jnp.max(a, axis=0, keepdims=True))
    p = e / jnp.sum(e, axis=0, keepdims=True)
    cum = p[0]
    for n in range(1, layer_j + 1):
        cum = cum + p[n]
    lb = jnp.clip(cum - p[0], 0.0, LB_MAX)

    def gates(n, span):
        base = chunk_base(n)
        rows = pl.ds(base, chunk)
        for head in range(heads):
            for direction in range(2):
                span = jnp.maximum(span, _gate_chunk(
                    z_refs[direction][rows, lanes(head)], lb[direction:direction + 1, lanes(head)],
                    sums_ref[direction], k_ref.at[direction, head], c_ref.at[direction, head], base))
        return span

    unroll = 2 if n_chunks % 2 == 0 else 1
    span = lax.fori_loop(0, n_chunks, gates, jnp.zeros((1, dk), F32), unroll=unroll)
    safe = jnp.max(span) <= SAFE_EXP2

    for head in range(heads):
        for direction in range(2):
            s_ref[direction, head] = refs[9][direction, head] if has_s0 else jnp.zeros((dk, dk), F32)

    def scan(factorised):
        def body(n, carry):
            for head in range(heads):
                for direction in range(2):
                    base = chunk_base(n if direction == 0 else n_chunks - 1 - n)
                    rows = pl.ds(base, chunk)
                    c_dir = c_ref.at[direction, head]
                    o_dir_ref[direction, head, rows, :] = _scan_chunk(
                        q_ref[rows, lanes(head)], k_ref[direction, head, rows, :], c_dir[rows, :],
                        v_ref[rows, lanes(head)], lambda r, c_dir=c_dir, base=base: c_dir[pl.ds(base + r, 1), :],
                        s_ref.at[direction, head], code_ref[direction], n_levels, direction == 0, factorised)
            return carry

        lax.fori_loop(0, n_chunks, body, 0, unroll=unroll if factorised else 1)

    lax.cond(safe, lambda: scan(True), lambda: scan(False))

    for head in range(heads):
        o = o_dir_ref[0, head] + o_dir_ref[1, head]
        o_ref[:, lanes(head)] = _rms(o, gn_ref[...]) * _silu(g_ref[:, lanes(head)])
        if emit_state:
            so_ref = refs[n_in + 1]
            if n_prior is not None:
                for n in range(n_prior):
                    so_ref[n, :, head] = refs[9 + int(has_s0) + n][:, head]
                so_ref = so_ref.at[n_prior]
            for direction in range(2):
                so_ref[direction, head] = s_ref[direction, head]


def _state_spec(layer_j, heads):
    return pl.BlockSpec((None, None, 2, heads, LANES, LANES), lambda b, h: (b, layer_j, 0, h, 0, 0))


def _hgrn(proj, row_block0, seq, n_batch, a_lb, gnorm, layer_j, s0=None, emit_state=False, prior=None):
    dk = LANES
    heads = SCAN_HEADS_PER_STEP
    groups = A_HEADS // heads
    chunk = min(SCAN_CHUNK, seq)
    codes, sums, n_levels = _scan_constants(chunk)
    n_layers = a_lb.shape[0]

    def col(section):
        return pl.BlockSpec((seq, heads * dk), lambda b, h: (row_block0 + b, section * groups + h))

    def const(a):
        return pl.BlockSpec(a.shape, lambda b, h: (0,) * a.ndim)

    layer_state_spec = pl.BlockSpec((None, 2, heads, dk, dk), lambda b, h: (b, 0, h, 0, 0))
    extra, extra_specs = [], []
    if s0 is not None:
        extra.append(s0)
        extra_specs.append(_state_spec(layer_j, heads))
    if prior is not None:
        extra += list(prior)
        extra_specs += [layer_state_spec] * len(prior)
    out_shape = [jax.ShapeDtypeStruct((n_batch * seq, A_HEADS * dk), F32)]
    out_specs = [pl.BlockSpec((seq, heads * dk), lambda b, h: (b, h))]
    if emit_state and prior is not None:
        stacked = len(prior) + 1
        out_shape.append(jax.ShapeDtypeStruct((n_batch, stacked, 2, A_HEADS, dk, dk), F32))
        out_specs.append(pl.BlockSpec((None, stacked, 2, heads, dk, dk), lambda b, h: (b, 0, 0, h, 0, 0)))
    elif emit_state:
        out_shape.append(jax.ShapeDtypeStruct((n_batch, 2, A_HEADS, dk, dk), F32))
        out_specs.append(layer_state_spec)
    per_head = (2, heads, seq, dk)
    return pl.pallas_call(
        functools.partial(_hgrn_kernel, layer_j=layer_j, chunk=chunk, n_levels=n_levels, heads=heads,
                          has_s0=s0 is not None, n_prior=None if prior is None else len(prior),
                          emit_state=emit_state),
        out_shape=tuple(out_shape),
        grid=(n_batch, groups),
        in_specs=[col(0), col(1), col(2), col(3), col(4),
                  pl.BlockSpec((n_layers, 2, heads * dk), lambda b, h: (0, 0, h)),
                  pl.BlockSpec((1, dk), lambda b, h: (0, 0)), const(codes), const(sums)] + extra_specs,
        out_specs=tuple(out_specs),
        scratch_shapes=[pltpu.VMEM((2, heads, dk, dk), F32), pltpu.VMEM(per_head, F32),
                        pltpu.VMEM(per_head, F32), pltpu.VMEM(per_head, F32)],
        compiler_params=_params("parallel", "parallel"),
        name="hgrn2_scan",
    )(proj, proj, proj, proj, proj, a_lb, gnorm, codes, sums, *extra)


def _head_lanes(shape, head):
    lane = lax.broadcasted_iota(jnp.int32, shape, len(shape) - 1)
    half = shape[-1] // 2
    return (lane < half) if head == 0 else (lane >= half)


def _ctx_attn_kernel(*refs, scale, n_prior):
    q_ref, k_ref, v_ref = refs[:3]
    o_ref, ko_ref, vo_ref = refs[-3:]
    dh = LANES // 2
    if n_prior is not None:
        for n in range(n_prior):
            ko_ref[n] = refs[3 + n][...]
            vo_ref[n] = refs[3 + n_prior + n][...]
        ko_ref, vo_ref = ko_ref.at[n_prior], vo_ref.at[n_prior]
    for pair in range(q_ref.shape[1] // LANES):
        lanes = slice(pair * LANES, (pair + 1) * LANES)
        k32 = k_ref[:, lanes]
        v32 = v_ref[:, lanes]
        k = k32.astype(BF16)
        v = v32.astype(BF16)
        q = q_ref[:, lanes]
        outs = []
        for head in range(2):
            ko_ref[2 * pair + head] = k32[:, head * dh:(head + 1) * dh]
            vo_ref[2 * pair + head] = v32[:, head * dh:(head + 1) * dh]
            qh = jnp.where(_head_lanes(q.shape, head), q, 0.0).astype(BF16)
            s = _dot_nt(qh, k) * scale
            p = jnp.exp(s - jnp.max(s, axis=-1, keepdims=True))
            outs.append(_dot(p.astype(BF16), v) / jnp.sum(p, axis=-1, keepdims=True))
        o_ref[:, lanes] = jnp.where(_head_lanes(q.shape, 0), outs[0], outs[1])


def _ctx_attention(proj, n_batch, seq, col0, prior, last):
    width = B_HEADS // 2 * LANES
    dh = LANES // 2
    cb = col0 // width
    assert col0 % width == 0
    layer_spec = pl.BlockSpec((None, B_HEADS, seq, dh), lambda b: (b, 0, 0, 0))
    if last:
        n_layers = len(prior) + 1
        cache = jax.ShapeDtypeStruct((n_batch, n_layers, B_HEADS, seq, dh), F32)
        cache_spec = pl.BlockSpec((None, n_layers, B_HEADS, seq, dh), lambda b: (b, 0, 0, 0, 0))
        earlier = [kv[0] for kv in prior] + [kv[1] for kv in prior]
    else:
        cache = jax.ShapeDtypeStruct((n_batch, B_HEADS, seq, dh), F32)
        cache_spec = layer_spec
        earlier = []

    def col(section):
        return pl.BlockSpec((seq, width), lambda b: (b, cb + section))

    return pl.pallas_call(
        functools.partial(_ctx_attn_kernel, scale=1.0 / math.sqrt(dh), n_prior=len(prior) if last else None),
        out_shape=(jax.ShapeDtypeStruct((n_batch * seq, width), F32), cache, cache),
        grid=(n_batch,),
        in_specs=[col(0), col(1), col(2)] + [layer_spec] * len(earlier),
        out_specs=(pl.BlockSpec((seq, width), lambda b: (b, 0)), cache_spec, cache_spec),
        compiler_params=_params("parallel"),
        name="context_attention",
    )(proj, proj, proj, *earlier)


def _rpb_tiles_kernel(rpb_ref, o_ref):
    h = pl.program_id(0)
    qc = lax.broadcasted_iota(jnp.int32, (GRID_W, GRID_W), 0)
    kc = lax.broadcasted_iota(jnp.int32, (GRID_W, GRID_W), 1)
    c0 = jnp.clip(qc - NA_COLS // 2, 0, GRID_W - NA_COLS)
    col_ok = (kc >= c0) & (kc < c0 + NA_COLS)
    dc = jnp.clip(kc - qc, -(NA_COLS - 1), NA_COLS - 1) + NA_COLS - 1
    for dr in range(2 * NA_ROWS - 1):
        t = jnp.zeros((GRID_W, GRID_W), F32)
        for i in range(2 * NA_COLS - 1):
            t = jnp.where(dc == i, rpb_ref[h, dr, i], t)
        o_ref[dr] = jnp.where(col_ok, t, MASKED)


def _rpb_tiles(rpb):
    heads, n_dr, n_dc = rpb.shape
    return pl.pallas_call(
        _rpb_tiles_kernel,
        out_shape=jax.ShapeDtypeStruct((heads, n_dr, GRID_W, GRID_W), F32),
        grid=(heads,),
        in_specs=[pl.BlockSpec(memory_space=pltpu.SMEM)],
        out_specs=pl.BlockSpec((None, n_dr, GRID_W, GRID_W), lambda h: (h, 0, 0, 0)),
        compiler_params=_params("parallel"),
        name="rpb_tiles",
    )(rpb)


def _nbr_geometry(grid_rows):
    win = min(NA_ROWS, grid_rows)
    blocks = []
    for qb in range(grid_rows // Q_ROWS):
        r0 = qb * Q_ROWS
        key0 = min(max(r0 - win // 2, 0), grid_rows - KEY_ROWS)
        tiles = []
        for i in range(Q_ROWS):
            r = r0 + i
            start = min(max(r - win // 2, 0), grid_rows - win)
            assert key0 <= start and start + win <= key0 + KEY_ROWS
            tiles.append([(key0 + jr - r + NA_ROWS - 1) if start <= key0 + jr < start + win else None
                          for jr in range(KEY_ROWS)])
        blocks.append((key0, tiles))
    return blocks


def _nbr_attn_kernel(q_ref, k_ref, v_ref, ck_ref, cv_ref, tiles_ref, o_ref, bias_ref, *, scale, geometry):
    w = GRID_W
    nq = Q_ROWS * w
    nk = KEY_ROWS * w

    @pl.when(pl.program_id(1) == 0)
    def _():
        masked = jnp.full((w, w), MASKED, F32)
        for head in range(2):
            for qb, (_, tiles) in enumerate(geometry):
                for i, row_tiles in enumerate(tiles):
                    for jr, dr in enumerate(row_tiles):
                        tile = masked if dr is None else tiles_ref[head, dr]
                        bias_ref[head, qb, i * w:(i + 1) * w, jr * w:(jr + 1) * w] = tile

    ck = ck_ref[...].astype(BF16)
    cv = cv_ref[...].astype(BF16)
    for qb, (key0, _) in enumerate(geometry):
        q = q_ref[qb * nq:(qb + 1) * nq, :] * scale
        k = k_ref[key0 * w:key0 * w + nk, :].astype(BF16)
        v = v_ref[key0 * w:key0 * w + nk, :].astype(BF16)
        outs = []
        for head in range(2):
            qh = jnp.where(_head_lanes(q.shape, head), q, 0.0).astype(BF16)
            s_loc = _dot_nt(qh, k) + bias_ref[head, qb]
            s_ctx = _dot_nt(qh, ck)
            m = jnp.maximum(jnp.max(s_loc, axis=-1, keepdims=True), jnp.max(s_ctx, axis=-1, keepdims=True))
            p_loc = jnp.exp(s_loc - m)
            p_ctx = jnp.exp(s_ctx - m)
            denom = jnp.sum(p_loc, axis=-1, keepdims=True) + jnp.sum(p_ctx, axis=-1, keepdims=True)
            outs.append((_dot(p_loc.astype(BF16), v) + _dot(p_ctx.astype(BF16), cv)) / denom)
        o_ref[qb * nq:(qb + 1) * nq, :] = jnp.where(_head_lanes(q.shape, 0), outs[0], outs[1])


def _nbr_attention(proj, row_block0, seq, n_batch, col0, ck, cv, tiles):
    pairs = B_HEADS // 2
    cb = col0 // LANES
    past = ck.shape[2]
    grid_rows = seq // GRID_W
    assert grid_rows % Q_ROWS == 0 and grid_rows >= KEY_ROWS
    geometry = _nbr_geometry(grid_rows)
    nq, nk = Q_ROWS * GRID_W, KEY_ROWS * GRID_W

    def col(section):
        return pl.BlockSpec((seq, LANES), lambda h, b: (row_block0 + b, cb + section * pairs + h))

    return pl.pallas_call(
        functools.partial(_nbr_attn_kernel, scale=1.0 / math.sqrt(LANES // 2), geometry=geometry),
        out_shape=jax.ShapeDtypeStruct((n_batch * seq, pairs * LANES), F32),
        grid=(pairs, n_batch),
        in_specs=[col(0), col(1), col(2),
                  pl.BlockSpec((None, None, past, LANES), lambda h, b: (b, h, 0, 0)),
                  pl.BlockSpec((None, None, past, LANES), lambda h, b: (b, h, 0, 0)),
                  pl.BlockSpec((2,) + tiles.shape[1:], lambda h, b: (h, 0, 0, 0))],
        out_specs=pl.BlockSpec((seq, LANES), lambda h, b: (b, h)),
        scratch_shapes=[pltpu.VMEM((2, len(geometry), nq, nk), F32)],
        compiler_params=_params("arbitrary", "arbitrary"),
        name="neighbourhood_attention",
    )(proj, proj, proj, ck, cv, tiles)


def _head_pairs(t):
    b, h, l, dh = t.shape
    return t.reshape(b, h // 2, 2, l, dh).transpose(0, 1, 3, 2, 4).reshape(b, h // 2, l, 2 * dh)


def kernel(x_prompt, x_sample, c, state_hgrn, cache_k, cache_v, c_ctx, mod_w, mod_b, norm_pre, norm_post,
           ffn_w_in, ffn_w_out, ab_w_in, ab_w_out, a_lb, a_gnorm, b_rpb, cv_w_in, cv_conv, cv_w_out):
    n_ctx, ctx_len, d = x_prompt.shape
    n_lat, lat_len, _ = x_sample.shape
    depth = mod_w.shape[0]
    d_a = a_lb.shape[-1]
    rows = _Rows(n_ctx * ctx_len, ctx_len, n_lat * lat_len, lat_len, ROW_TILE)
    ffn_rows = _Rows(n_ctx * ctx_len, ctx_len, n_lat * lat_len, lat_len, FFN_ROW_TILE)
    fused_rows = _Rows(n_ctx * ctx_len, ctx_len, n_lat * lat_len, lat_len, FUSED_FFN_ROW_TILE)
    assert (n_ctx * ctx_len) % lat_len == 0
    lat_block0 = (n_ctx * ctx_len) // lat_len

    x = (x_prompt.reshape(-1, d), x_sample.reshape(-1, d))

    cond = jnp.concatenate([c_ctx[None, :], c], axis=0)
    cond = jnp.pad(cond, ((0, -cond.shape[0] % 8), (0, 0)))
    mod = _modulation(cond, mod_w, mod_b)
    mod = mod.reshape(depth, cond.shape[0], 1, N_MOD * d)

    ffn_w_out = ffn_w_out.astype(BF16)
    ab_w_in = ab_w_in.astype(BF16)
    ab_w_out = ab_w_out.astype(BF16)
    cv_w_in = cv_w_in.astype(BF16)
    cv_w_out = cv_w_out.astype(BF16)

    n_even = a_lb.shape[0]
    layer_states, layer_caches = [], []
    for l in range(depth):
        j = l // 2

        def gains(t, sub):
            return t[l, sub][None, :]

        x = _ffn(x, mod[l], gains(norm_pre, 0), gains(norm_post, 0), (ffn_w_in, (l, 0)), (ffn_w_out, (l, 0)),
                 ffn_rows, 0)
        if l % 2 == 0:
            proj = _inproj(x, mod[l], gains(norm_pre, 1), (ab_w_in, (j,)), rows)
            gn = a_gnorm[j][None, :]
            last = j == n_even - 1
            oa_ctx, new_state = _hgrn(proj, 0, ctx_len, n_ctx, a_lb, gn, j, emit_state=True,
                                      prior=layer_states if last else None)
            oa_lat, = _hgrn(proj, lat_block0, lat_len, n_lat, a_lb, gn, j, s0=state_hgrn)
            ob_ctx, *new_caches = _ctx_attention(proj, n_ctx, ctx_len, 5 * d_a, layer_caches, last)
            layer_states.append(new_state)
            layer_caches.append(new_caches)
            ob_lat = _nbr_attention(proj, lat_block0, lat_len, n_lat, 5 * d_a,
                                    _head_pairs(cache_k[:, j]), _head_pairs(cache_v[:, j]),
                                    _rpb_tiles(b_rpb[j]))
            mixer = ((oa_ctx, oa_lat), (ob_ctx, ob_lat), (ab_w_out, (j,)), gains(norm_post, 1))
        else:
            x = _conv_mixer(x, mod[l], gains(norm_pre, 1), gains(norm_post, 1), (cv_w_in, (j,)), cv_conv[j],
                            (cv_w_out, (j,)), rows)
            mixer = None
        x = _ffn(x, mod[l], gains(norm_pre, 2), gains(norm_post, 2), (ffn_w_in, (l, 1)), (ffn_w_out, (l, 1)),
                 ffn_rows if mixer is None else fused_rows, 2, split_out=(l == depth - 1), mixer=mixer)

    return (x[0].reshape(x_prompt.shape), x[1].reshape(x_sample.shape), new_state, *new_caches)
```

```python
import functools
import math

import jax
import jax.numpy as jnp
import numpy as np
from jax import lax
from jax.experimental import pallas as pl
from jax.experimental.pallas import tpu as pltpu

F32 = jnp.float32
BF16 = jnp.bfloat16

EPS = 1e-6
LB_MAX = 1.0 - 1e-4
MASKED = -1e30

GRID_W = 64
A_HEADS = 4
B_HEADS = 8
NA_ROWS = 8
NA_COLS = 16
CONV_W = 3
N_MOD = 9

VMEM_LIMIT_BYTES = 56 * 1024 * 1024
LANES = 128

ROW_TILE = 512
FFN_ROW_TILE = 512
FFN_SUB_TILES = 2
FUSED_FFN_ROW_TILE = 512
HALO = 16
SCAN_CHUNK = 128
SCAN_GROUP = 16
SCAN_HEADS_PER_STEP = 2
SAFE_EXP2 = 115.0
LOG2_E = math.log2(math.e)
Q_ROWS = 4
KEY_ROWS = 12


def _params(*sem):
    return pltpu.CompilerParams(dimension_semantics=sem, vmem_limit_bytes=VMEM_LIMIT_BYTES)


def _rms(x, g):
    return x * lax.rsqrt(jnp.mean(x * x, axis=-1, keepdims=True) + EPS) * g


def _silu(x):
    return x * jax.nn.sigmoid(x)


def _dot(a, b):
    return jnp.dot(a, b, preferred_element_type=F32)


def _dot_nt(a, b):
    return lax.dot_general(a, b, (((1,), (1,)), ((), ())), preferred_element_type=F32)


def _dot_tn(a, b):
    return lax.dot_general(a, b, (((0,), (0,)), ((), ())), preferred_element_type=F32)


def _resident_spec(w, index):
    lead = len(index)
    zeros = (0,) * (w.ndim - lead)
    return pl.BlockSpec((None,) * lead + tuple(w.shape[lead:]), lambda *_: tuple(index) + zeros,
                        pipeline_mode=pl.Buffered(1))


def _mod_parts(mod, j, d):
    return (mod[:, (3 * j) * d:(3 * j + 1) * d],
            mod[:, (3 * j + 1) * d:(3 * j + 2) * d],
            mod[:, (3 * j + 2) * d:(3 * j + 3) * d])


class _Rows:
    def __init__(self, n_ctx_rows, ctx_len, n_lat_rows, lat_len, tile):
        assert ctx_len % tile == 0 or tile % ctx_len == 0
        assert lat_len % tile == 0 and n_ctx_rows % tile == 0 and n_lat_rows % tile == 0
        self.tile = tile
        self.n_ctx_tiles = n_ctx_rows // tile
        self.n_tiles = (n_ctx_rows + n_lat_rows) // tile
        self.lat_tiles_per_batch = lat_len // tile
        self.ctx_len = ctx_len
        self.lat_len = lat_len

    def mod_row(self, i):
        return jnp.where(i < self.n_ctx_tiles, 0, 1 + (i - self.n_ctx_tiles) // self.lat_tiles_per_batch)

    def group_specs(self, width):
        nc = self.n_ctx_tiles
        return [pl.BlockSpec((self.tile, width), lambda i: (jnp.minimum(i, nc - 1), 0)),
                pl.BlockSpec((self.tile, width), lambda i: (jnp.maximum(i - nc, 0), 0))]

    def mod_spec(self, width):
        return pl.BlockSpec((None, 1, width), lambda i: (self.mod_row(i), 0, 0))


def _mod_kernel(cond_ref, w_ref, b_ref, o_ref):
    s = _silu(cond_ref[...]).astype(BF16)
    o_ref[...] = _dot(s, w_ref[...].astype(BF16)) + b_ref[...]


def _modulation(cond, mod_w, mod_b):
    depth, d, n = mod_w.shape
    rows = cond.shape[0]
    tn = 1536
    assert n % tn == 0
    return pl.pallas_call(
        _mod_kernel,
        out_shape=jax.ShapeDtypeStruct((depth, rows, n), F32),
        grid=(depth, n // tn),
        in_specs=[pl.BlockSpec((rows, d), lambda l, j: (0, 0)),
                  pl.BlockSpec((None, d, tn), lambda l, j: (l, 0, j)),
                  pl.BlockSpec((None, 1, tn), lambda l, j: (l, 0, j))],
        out_specs=pl.BlockSpec((None, rows, tn), lambda l, j: (l, 0, j)),
        compiler_params=_params("parallel", "parallel"),
        name="adaln_mod",
    )(cond, mod_w, mod_b.reshape(depth, 1, n))


def _ffn_kernel(*refs, j, tf, n_sub, n_ctx_tiles, split_in, split_out, fuse_mixer):
    n_x = 2 if split_in else 1
    x_refs = refs[:n_x]
    n_mix = 6 if fuse_mixer else 0
    mix_refs = refs[n_x:n_x + n_mix]
    mod_ref, gpre_ref, gpost_ref, win_ref, wout_ref = refs[n_x + n_mix:n_x + n_mix + 5]
    n_scratch = 3 if fuse_mixer else 2
    o_refs = refs[n_x + n_mix + 5:-n_scratch]
    h_ref, act_ref = refs[-n_scratch:][:2]
    is_ctx = pl.program_id(0) < n_ctx_tiles

    def load_x(rows):
        if split_in:
            return jnp.where(is_ctx, x_refs[0][rows, :], x_refs[1][rows, :])
        return x_refs[0][rows, :]

    tm, d = x_refs[0].shape
    dff = wout_ref.shape[0]
    shift, scale, gate = _mod_parts(mod_ref[...], j, d)
    sub_rows = [slice(sub * (tm // n_sub), (sub + 1) * (tm // n_sub)) for sub in range(n_sub)]
    for rows in sub_rows:
        x_in = load_x(rows)
        if fuse_mixer:
            ac_ref, al_ref, bc_ref, bl_ref, wmix_ref, gmix_ref = mix_refs
            da = ac_ref.shape[1]
            a = jnp.where(is_ctx, ac_ref[rows, :], al_ref[rows, :]).astype(BF16)
            b = jnp.where(is_ctx, bc_ref[rows, :], bl_ref[rows, :]).astype(BF16)
            mixed = _dot(a, wmix_ref[:da, :]) + _dot(b, wmix_ref[da:, :])
            x_in = x_in + _mod_parts(mod_ref[...], 1, d)[2] * _rms(mixed, gmix_ref[...])
            refs[-1][rows, :] = x_in
        h_ref[rows, :] = (_rms(x_in, gpre_ref[...]) * (1.0 + scale) + shift).astype(BF16)
    for c in range(dff // tf):
        w_gate = win_ref[:, c * tf:(c + 1) * tf].astype(BF16)
        w_up = win_ref[:, dff + c * tf:dff + (c + 1) * tf].astype(BF16)
        for rows in sub_rows:
            g = _dot(h_ref[rows, :], w_gate)
            u = _dot(h_ref[rows, :], w_up)
            act_ref[rows, c * tf:(c + 1) * tf] = (_silu(g) * u).astype(BF16)
    for rows in sub_rows:
        out = _dot(act_ref[rows, :], wout_ref[...])
        x_in = refs[-1][rows, :] if fuse_mixer else load_x(rows)
        y = x_in + (0.5 * gate) * _rms(out, gpost_ref[...])
        if split_out:
            @pl.when(is_ctx)
            def _():
                o_refs[0][rows, :] = y

            @pl.when(jnp.logical_not(is_ctx))
            def _():
                o_refs[1][rows, :] = y
        else:
            o_refs[0][rows, :] = y


def _ffn(x, mod_l, g_pre, g_post, w_in, w_out, rows, j, split_out=False, mixer=None):
    split_in = isinstance(x, tuple)
    n_sub = FFN_SUB_TILES if rows.tile >= FFN_ROW_TILE else 1
    mix_args, mix_specs = [], []
    if mixer is not None:
        o_a, o_b, w_mix, g_mix = mixer
        mix_args = [*o_a, *o_b, w_mix[0], g_mix]
        mix_specs = (rows.group_specs(o_a[0].shape[1]) + rows.group_specs(o_b[0].shape[1])
                     + [_resident_spec(*w_mix), pl.BlockSpec((1, g_mix.shape[1]), lambda i: (0, 0))])
    d = x[0].shape[1] if split_in else x.shape[1]
    dff = w_out[0].shape[-2]
    tm = rows.tile
    tf = 256
    assert dff % tf == 0
    out_shape = jax.ShapeDtypeStruct((rows.n_tiles * tm, d), F32)
    out_spec = pl.BlockSpec((tm, d), lambda i: (i, 0))
    if split_out:
        out_shape = (jax.ShapeDtypeStruct((rows.n_ctx_tiles * tm, d), F32),
                     jax.ShapeDtypeStruct(((rows.n_tiles - rows.n_ctx_tiles) * tm, d), F32))
        out_spec = tuple(rows.group_specs(d))
    return pl.pallas_call(
        functools.partial(_ffn_kernel, j=j, tf=tf, n_sub=n_sub, n_ctx_tiles=rows.n_ctx_tiles,
                          split_in=split_in, split_out=split_out, fuse_mixer=mixer is not None),
        out_shape=out_shape,
        grid=(rows.n_tiles,),
        in_specs=(rows.group_specs(d) if split_in else [pl.BlockSpec((tm, d), lambda i: (i, 0))]) + mix_specs + [
                  rows.mod_spec(mod_l.shape[-1]),
                  pl.BlockSpec((1, d), lambda i: (0, 0)),
                  pl.BlockSpec((1, d), lambda i: (0, 0)),
                  _resident_spec(*w_in), _resident_spec(*w_out)],
        out_specs=out_spec,
        scratch_shapes=[pltpu.VMEM((tm, d), BF16), pltpu.VMEM((tm, dff), BF16)]
                       + ([pltpu.VMEM((tm, d), F32)] if mixer is not None else []),
        compiler_params=_params("arbitrary" if split_out else "parallel"),
        name="swiglu_sublayer",
    )(*(x if split_in else (x,)), *mix_args, mod_l, g_pre, g_post, w_in[0], w_out[0])


def _inproj_kernel(x_ref, mod_ref, gpre_ref, w_ref, o_ref, h_ref, *, tn):
    d = x_ref.shape[1]
    shift, scale, _ = _mod_parts(mod_ref[...], 1, d)
    h_ref[...] = (_rms(x_ref[...], gpre_ref[...]) * (1.0 + scale) + shift).astype(BF16)
    for c in range(w_ref.shape[1] // tn):
        o_ref[:, c * tn:(c + 1) * tn] = _dot(h_ref[...], w_ref[:, c * tn:(c + 1) * tn].astype(BF16))


def _inproj(x, mod_l, g_pre, w, rows):
    m, d = x.shape
    n = w[0].shape[-1]
    tm = rows.tile
    tn = 512
    assert n % tn == 0
    return pl.pallas_call(
        functools.partial(_inproj_kernel, tn=tn),
        out_shape=jax.ShapeDtypeStruct((m, n), F32),
        grid=(rows.n_tiles,),
        in_specs=[pl.BlockSpec((tm, d), lambda i: (i, 0)),
                  rows.mod_spec(mod_l.shape[-1]),
                  pl.BlockSpec((1, d), lambda i: (0, 0)),
                  _resident_spec(*w)],
        out_specs=pl.BlockSpec((tm, n), lambda i: (i, 0)),
        scratch_shapes=[pltpu.VMEM((tm, d), BF16)],
        compiler_params=_params("parallel"),
        name="mixer_in_proj",
    )(x, mod_l, g_pre, w[0])


def _conv_kernel(x_ref, xp_ref, xn_ref, mod_ref, gpre_ref, gpost_ref, win_ref, cw_ref, wout_ref, o_ref,
                 h_ref, u_ref, z_ref, *, tc, rows):
    tm, d = x_ref.shape
    i = pl.program_id(0)
    shift, scale, gate = _mod_parts(mod_ref[...], 1, d)
    g_pre = gpre_ref[...]

    def pre(x):
        return (_rms(x, g_pre) * (1.0 + scale) + shift).astype(BF16)

    h_ref[0:HALO, :] = pre(xp_ref[...])
    h_ref[HALO:HALO + tm, :] = pre(x_ref[...])
    h_ref[HALO + tm:, :] = pre(xn_ref[...])

    seq_len = jnp.where(i < rows.n_ctx_tiles, rows.ctx_len, rows.lat_len)
    pos = (i * tm + lax.broadcasted_iota(jnp.int32, (tm, 1), 0)) & (seq_len - 1)
    has_left = pos != 0
    has_right = pos != seq_len - 1

    for c in range(d // tc):
        cols = slice(c * tc, (c + 1) * tc)
        b_gate = _dot(h_ref[HALO:HALO + tm, :], win_ref[:, c * tc:(c + 1) * tc].astype(BF16))
        c_gate = _dot(h_ref[...], win_ref[:, d + c * tc:d + (c + 1) * tc].astype(BF16))
        x_in = _dot(h_ref[...], win_ref[:, 2 * d + c * tc:2 * d + (c + 1) * tc].astype(BF16))
        u_ref[...] = c_gate * x_in
        cw = cw_ref[:, cols]
        y = (cw[0:1, :] * jnp.where(has_left, u_ref[HALO - 1:HALO - 1 + tm, :], 0.0)
             + cw[1:2, :] * u_ref[HALO:HALO + tm, :]
             + cw[2:3, :] * jnp.where(has_right, u_ref[HALO + 1:HALO + 1 + tm, :], 0.0))
        z_ref[:, cols] = (b_gate * y).astype(BF16)
    out = _dot(z_ref[...], wout_ref[...])
    o_ref[...] = x_ref[...] + gate * _rms(out, gpost_ref[...])


def _conv_mixer(x, mod_l, g_pre, g_post, w_in, conv_w, w_out, rows):
    m, d = x.shape
    tm = rows.tile
    tc = 256
    assert rows.ctx_len & (rows.ctx_len - 1) == 0 and rows.lat_len & (rows.lat_len - 1) == 0
    hb = tm // HALO
    last = m // HALO - 1
    return pl.pallas_call(
        functools.partial(_conv_kernel, tc=tc, rows=rows),
        out_shape=jax.ShapeDtypeStruct((m, d), F32),
        grid=(rows.n_tiles,),
        in_specs=[pl.BlockSpec((tm, d), lambda i: (i, 0)),
                  pl.BlockSpec((HALO, d), lambda i: (jnp.maximum(i * hb - 1, 0), 0)),
                  pl.BlockSpec((HALO, d), lambda i: (jnp.minimum((i + 1) * hb, last), 0)),
                  rows.mod_spec(mod_l.shape[-1]),
                  pl.BlockSpec((1, d), lambda i: (0, 0)),
                  pl.BlockSpec((1, d), lambda i: (0, 0)),
                  _resident_spec(*w_in),
                  pl.BlockSpec((CONV_W, d), lambda i: (0, 0)),
                  _resident_spec(*w_out)],
        out_specs=pl.BlockSpec((tm, d), lambda i: (i, 0)),
        scratch_shapes=[pltpu.VMEM((tm + 2 * HALO, d), BF16),
                        pltpu.VMEM((tm + 2 * HALO, tc), F32),
                        pltpu.VMEM((tm, d), BF16)],
        compiler_params=_params("parallel"),
        name="conv_mixer",
    )(x, x, x, mod_l, g_pre, g_post, w_in[0], conv_w, w_out[0])


def _scan_constants(c):
    t, s = np.meshgrid(np.arange(c), np.arange(c), indexing="ij")
    x = t ^ s
    n_levels = int(math.log2(c // SCAN_GROUP))
    code = np.full((c, c), n_levels, np.int32)
    for lvl in range(n_levels - 1, 0, -1):
        code = np.where(x >= (c >> lvl), lvl, code)
    code = np.where(x < SCAN_GROUP, n_levels + 1, code)
    visited = np.stack([s <= t, s >= t])
    codes = np.where(visited, code[None], 0).astype(np.int32)
    return jnp.asarray(codes), jnp.asarray(visited, BF16), n_levels


def _gate_chunk(z, lb, cum, k_ref, c_ref, base):
    c_len, dk = z.shape
    e_abs = jnp.exp(-jnp.abs(z))
    inv = 1.0 / (1.0 + e_abs)
    sig = jnp.where(z >= 0, inv, e_abs * inv)
    sig_neg = jnp.where(z >= 0, e_abs * inv, inv)
    f = lb + (1.0 - lb) * sig
    log_sig = jnp.minimum(z, 0.0) + jnp.log(inv)
    log_f = jnp.maximum(jnp.log1p(-lb) + log_sig, jnp.log(jnp.maximum(f, 1e-30))) * LOG2_E
    k_ref[pl.ds(base, c_len), :] = (1.0 - lb) * sig_neg

    hi = log_f.astype(BF16)
    rest = log_f - hi.astype(F32)
    mid = rest.astype(BF16)
    lo = (rest - mid.astype(F32)).astype(BF16)
    sums = _dot(cum, jnp.concatenate([hi, mid, lo], axis=1))
    c = sums[:, :dk] + sums[:, dk:2 * dk] + sums[:, 2 * dk:]
    c_ref[pl.ds(base, c_len), :] = c

    span = jnp.zeros((1, dk), F32)
    for g in range(0, c_len, SCAN_GROUP):
        ends = c_ref[pl.ds(base + g, 1), :] - c_ref[pl.ds(base + g + SCAN_GROUP - 1, 1), :]
        span = jnp.maximum(span, jnp.abs(ends))
    return span


def _scan_chunk(q, k, c, v, c_row, st_ref, kt_ref, code, n_levels, forward, factorised):
    c_len, dk = q.shape
    q16 = q.astype(BF16)
    k16 = k.astype(BF16)

    def reference_rows(block, offset):
        parts = [jnp.broadcast_to(c_row(n * block + offset), (block, dk)) for n in range(c_len // block)]
        return parts[0] if len(parts) == 1 else jnp.concatenate(parts, axis=0)

    def scores(slot, q_scaled, k_scaled):
        kt_ref[slot] = k_scaled.T
        return _dot(q_scaled, kt_ref[slot])

    att = jnp.zeros((c_len, c_len), F32)
    for lvl in range(1, n_levels + 1):
        half = c_len >> lvl
        e = jnp.exp2(-jnp.abs(c - reference_rows(2 * half, half))).astype(BF16)
        att = jnp.where(code == lvl, scores(lvl, q16 * e, k16 * e), att)

    if factorised:
        dq = c - reference_rows(SCAN_GROUP, 0 if forward else SCAN_GROUP - 1)
        same = scores(0, q16 * jnp.exp2(dq).astype(BF16), k16 * jnp.exp2(-dq).astype(BF16))
    else:
        row = lax.broadcasted_iota(jnp.int32, (c_len, dk), 0)
        t = lax.broadcasted_iota(jnp.int32, (c_len, c_len), 0)
        s = lax.broadcasted_iota(jnp.int32, (c_len, c_len), 1)
        pos = row & (SCAN_GROUP - 1)
        same = jnp.zeros((c_len, c_len), F32)
        for dist in range(SCAN_GROUP):
            if dist == 0:
                k_s, c_s = k, c
            else:
                shift = dist if forward else c_len - dist
                k_s, c_s = pltpu.roll(k, shift, 0), pltpu.roll(c, shift, 0)
            valid = (pos >= dist) if forward else (pos + dist < SCAN_GROUP)
            term = q * k_s * jnp.exp2(jnp.where(valid, c - c_s, 0.0))
            a = jnp.sum(jnp.where(valid, term, 0.0), axis=-1, keepdims=True)
            same = jnp.where(s == (t - dist if forward else t + dist), a, same)
    att = jnp.where(code == n_levels + 1, same, att)

    st = st_ref[...]
    v16 = v.astype(BF16)
    o = _dot(att.astype(BF16), v16) + _dot(q16 * jnp.exp2(c).astype(BF16), st.astype(BF16))
    c_end = c_row(c_len - 1 if forward else 0)
    row_decay = jnp.broadcast_to(jnp.exp2(c_end), (dk, dk)).T
    st_ref[...] = st * row_decay + _dot_tn(k16 * jnp.exp2(c_end - c).astype(BF16), v16)
    return o


def _hgrn_kernel(*refs, layer_j, chunk, n_levels, heads, has_s0, n_prior, emit_state):
    q_ref, zf_ref, zb_ref, v_ref, g_ref, alb_ref, gn_ref, code_ref, sums_ref = refs[:9]
    n_in = 9 + int(has_s0) + (n_prior or 0)
    o_ref = refs[n_in]
    s_ref, o_dir_ref, k_ref, c_ref, kt_ref = refs[-5:]
    seq = q_ref.shape[0]
    dk = q_ref.shape[1] // heads
    n_chunks = seq // chunk
    z_refs = (zf_ref, zb_ref)

    def chunk_base(n):
        return pl.multiple_of(n * chunk, chunk)

    def lanes(head):
        return slice(head * dk, (head + 1) * dk)

    a = alb_ref[...]
    e = jnp.exp(a - jnp.max(a, axis=0, keepdims=True))
    p = e / jnp.sum(e, axis=0, keepdims=True)
    cum = p[0]
    for n in range(1, layer_j + 1):
        cum = cum + p[n]
    lb = jnp.clip(cum - p[0], 0.0, LB_MAX)

    def gates(n, span):
        base = chunk_base(n)
        rows = pl.ds(base, chunk)
        for head in range(heads):
            for direction in range(2):
                span = jnp.maximum(span, _gate_chunk(
                    z_refs[direction][rows, lanes(head)], lb[direction:direction + 1, lanes(head)],
                    sums_ref[direction], k_ref.at[direction, head], c_ref.at[direction, head], base))
        return span

    unroll = 2 if n_chunks % 2 == 0 else 1
    span = lax.fori_loop(0, n_chunks, gates, jnp.zeros((1, dk), F32), unroll=unroll)
    safe = jnp.max(span) <= SAFE_EXP2

    for head in range(heads):
        for direction in range(2):
            s_ref[direction, head] = refs[9][direction, head] if has_s0 else jnp.zeros((dk, dk), F32)

    def scan(factorised):
        def body(n, carry):
            for head in range(heads):
                for direction in range(2):
                    base = chunk_base(n if direction == 0 else n_chunks - 1 - n)
                    rows = pl.ds(base, chunk)
                    c_dir = c_ref.at[direction, head]
                    o_dir_ref[direction, head, rows, :] = _scan_chunk(
                        q_ref[rows, lanes(head)], k_ref[direction, head, rows, :], c_dir[rows, :],
                        v_ref[rows, lanes(head)], lambda r, c_dir=c_dir, base=base: c_dir[pl.ds(base + r, 1), :],
                        s_ref.at[direction, head], kt_ref.at[direction, head], code_ref[direction], n_levels,
                        direction == 0, factorised)
            return carry

        lax.fori_loop(0, n_chunks, body, 0, unroll=unroll if factorised else 1)

    lax.cond(safe, lambda: scan(True), lambda: scan(False))

    for head in range(heads):
        o = o_dir_ref[0, head] + o_dir_ref[1, head]
        o_ref[:, lanes(head)] = _rms(o, gn_ref[...]) * _silu(g_ref[:, lanes(head)])
        if emit_state:
            so_ref = refs[n_in + 1]
            if n_prior is not None:
                for n in range(n_prior):
                    so_ref[n, :, head] = refs[9 + int(has_s0) + n][:, head]
                so_ref = so_ref.at[n_prior]
            for direction in range(2):
                so_ref[direction, head] = s_ref[direction, head]


def _state_spec(layer_j, heads):
    return pl.BlockSpec((None, None, 2, heads, LANES, LANES), lambda b, h: (b, layer_j, 0, h, 0, 0))


def _hgrn(proj, row_block0, seq, n_batch, a_lb, gnorm, layer_j, s0=None, emit_state=False, prior=None):
    dk = LANES
    heads = SCAN_HEADS_PER_STEP
    groups = A_HEADS // heads
    chunk = min(SCAN_CHUNK, seq)
    codes, sums, n_levels = _scan_constants(chunk)
    n_layers = a_lb.shape[0]

    def col(section):
        return pl.BlockSpec((seq, heads * dk), lambda b, h: (row_block0 + b, section * groups + h))

    def const(a):
        return pl.BlockSpec(a.shape, lambda b, h: (0,) * a.ndim)

    layer_state_spec = pl.BlockSpec((None, 2, heads, dk, dk), lambda b, h: (b, 0, h, 0, 0))
    extra, extra_specs = [], []
    if s0 is not None:
        extra.append(s0)
        extra_specs.append(_state_spec(layer_j, heads))
    if prior is not None:
        extra += list(prior)
        extra_specs += [layer_state_spec] * len(prior)
    out_shape = [jax.ShapeDtypeStruct((n_batch * seq, A_HEADS * dk), F32)]
    out_specs = [pl.BlockSpec((seq, heads * dk), lambda b, h: (b, h))]
    if emit_state and prior is not None:
        stacked = len(prior) + 1
        out_shape.append(jax.ShapeDtypeStruct((n_batch, stacked, 2, A_HEADS, dk, dk), F32))
        out_specs.append(pl.BlockSpec((None, stacked, 2, heads, dk, dk), lambda b, h: (b, 0, 0, h, 0, 0)))
    elif emit_state:
        out_shape.append(jax.ShapeDtypeStruct((n_batch, 2, A_HEADS, dk, dk), F32))
        out_specs.append(layer_state_spec)
    per_head = (2, heads, seq, dk)
    return pl.pallas_call(
        functools.partial(_hgrn_kernel, layer_j=layer_j, chunk=chunk, n_levels=n_levels, heads=heads,
                          has_s0=s0 is not None, n_prior=None if prior is None else len(prior),
                          emit_state=emit_state),
        out_shape=tuple(out_shape),
        grid=(n_batch, groups),
        in_specs=[col(0), col(1), col(2), col(3), col(4),
                  pl.BlockSpec((n_layers, 2, heads * dk), lambda b, h: (0, 0, h)),
                  pl.BlockSpec((1, dk), lambda b, h: (0, 0)), const(codes), const(sums)] + extra_specs,
        out_specs=tuple(out_specs),
        scratch_shapes=[pltpu.VMEM((2, heads, dk, dk), F32), pltpu.VMEM(per_head, F32),
                        pltpu.VMEM(per_head, F32), pltpu.VMEM(per_head, F32),
                        pltpu.VMEM((2, heads, n_levels + 1, dk, chunk), BF16)],
        compiler_params=_params("parallel", "parallel"),
        name="hgrn2_scan",
    )(proj, proj, proj, proj, proj, a_lb, gnorm, codes, sums, *extra)


def _head_lanes(shape, head):
    lane = lax.broadcasted_iota(jnp.int32, shape, len(shape) - 1)
    half = shape[-1] // 2
    return (lane < half) if head == 0 else (lane >= half)


def _ctx_attn_kernel(*refs, scale, n_prior):
    q_ref, k_ref, v_ref = refs[:3]
    o_ref, ko_ref, vo_ref = refs[-3:]
    dh = LANES // 2
    if n_prior is not None:
        for n in range(n_prior):
            ko_ref[n] = refs[3 + n][...]
            vo_ref[n] = refs[3 + n_prior + n][...]
        ko_ref, vo_ref = ko_ref.at[n_prior], vo_ref.at[n_prior]
    for pair in range(q_ref.shape[1] // LANES):
        lanes = slice(pair * LANES, (pair + 1) * LANES)
        k32 = k_ref[:, lanes]
        v32 = v_ref[:, lanes]
        k = k32.astype(BF16)
        v = v32.astype(BF16)
        q = q_ref[:, lanes]
        outs = []
        for head in range(2):
            ko_ref[2 * pair + head] = k32[:, head * dh:(head + 1) * dh]
            vo_ref[2 * pair + head] = v32[:, head * dh:(head + 1) * dh]
            qh = jnp.where(_head_lanes(q.shape, head), q, 0.0).astype(BF16)
            s = _dot_nt(qh, k) * scale
            p = jnp.exp(s - jnp.max(s, axis=-1, keepdims=True))
            outs.append(_dot(p.astype(BF16), v) / jnp.sum(p, axis=-1, keepdims=True))
        o_ref[:, lanes] = jnp.where(_head_lanes(q.shape, 0), outs[0], outs[1])


def _ctx_attention(proj, n_batch, seq, col0, prior, last):
    width = B_HEADS // 2 * LANES
    dh = LANES // 2
    cb = col0 // width
    assert col0 % width == 0
    layer_spec = pl.BlockSpec((None, B_HEADS, seq, dh), lambda b: (b, 0, 0, 0))
    if last:
        n_layers = len(prior) + 1
        cache = jax.ShapeDtypeStruct((n_batch, n_layers, B_HEADS, seq, dh), F32)
        cache_spec = pl.BlockSpec((None, n_layers, B_HEADS, seq, dh), lambda b: (b, 0, 0, 0, 0))
        earlier = [kv[0] for kv in prior] + [kv[1] for kv in prior]
    else:
        cache = jax.ShapeDtypeStruct((n_batch, B_HEADS, seq, dh), F32)
        cache_spec = layer_spec
        earlier = []

    def col(section):
        return pl.BlockSpec((seq, width), lambda b: (b, cb + section))

    return pl.pallas_call(
        functools.partial(_ctx_attn_kernel, scale=1.0 / math.sqrt(dh), n_prior=len(prior) if last else None),
        out_shape=(jax.ShapeDtypeStruct((n_batch * seq, width), F32), cache, cache),
        grid=(n_batch,),
        in_specs=[col(0), col(1), col(2)] + [layer_spec] * len(earlier),
        out_specs=(pl.BlockSpec((seq, width), lambda b: (b, 0)), cache_spec, cache_spec),
        compiler_params=_params("parallel"),
        name="context_attention",
    )(proj, proj, proj, *earlier)


def _rpb_tiles_kernel(rpb_ref, o_ref):
    h = pl.program_id(0)
    qc = lax.broadcasted_iota(jnp.int32, (GRID_W, GRID_W), 0)
    kc = lax.broadcasted_iota(jnp.int32, (GRID_W, GRID_W), 1)
    c0 = jnp.clip(qc - NA_COLS // 2, 0, GRID_W - NA_COLS)
    col_ok = (kc >= c0) & (kc < c0 + NA_COLS)
    dc = jnp.clip(kc - qc, -(NA_COLS - 1), NA_COLS - 1) + NA_COLS - 1
    for dr in range(2 * NA_ROWS - 1):
        t = jnp.zeros((GRID_W, GRID_W), F32)
        for i in range(2 * NA_COLS - 1):
            t = jnp.where(dc == i, rpb_ref[h, dr, i], t)
        o_ref[dr] = jnp.where(col_ok, t, MASKED)


def _rpb_tiles(rpb):
    heads, n_dr, n_dc = rpb.shape
    return pl.pallas_call(
        _rpb_tiles_kernel,
        out_shape=jax.ShapeDtypeStruct((heads, n_dr, GRID_W, GRID_W), F32),
        grid=(heads,),
        in_specs=[pl.BlockSpec(memory_space=pltpu.SMEM)],
        out_specs=pl.BlockSpec((None, n_dr, GRID_W, GRID_W), lambda h: (h, 0, 0, 0)),
        compiler_params=_params("parallel"),
        name="rpb_tiles",
    )(rpb)


def _nbr_geometry(grid_rows):
    win = min(NA_ROWS, grid_rows)
    blocks = []
    for qb in range(grid_rows // Q_ROWS):
        r0 = qb * Q_ROWS
        key0 = min(max(r0 - win // 2, 0), grid_rows - KEY_ROWS)
        tiles = []
        for i in range(Q_ROWS):
            r = r0 + i
            start = min(max(r - win // 2, 0), grid_rows - win)
            assert key0 <= start and start + win <= key0 + KEY_ROWS
            tiles.append([(key0 + jr - r + NA_ROWS - 1) if start <= key0 + jr < start + win else None
                          for jr in range(KEY_ROWS)])
        blocks.append((key0, tiles))
    return blocks


def _nbr_attn_kernel(q_ref, k_ref, v_ref, ck_ref, cv_ref, tiles_ref, o_ref, bias_ref, *, scale, geometry):
    w = GRID_W
    nq = Q_ROWS * w
    nk = KEY_ROWS * w

    @pl.when(pl.program_id(1) == 0)
    def _():
        masked = jnp.full((w, w), MASKED, F32)
        for head in range(2):
            for qb, (_, tiles) in enumerate(geometry):
                for i, row_tiles in enumerate(tiles):
                    for jr, dr in enumerate(row_tiles):
                        tile = masked if dr is None else tiles_ref[head, dr]
                        bias_ref[head, qb, i * w:(i + 1) * w, jr * w:(jr + 1) * w] = tile

    ck = ck_ref[...].astype(BF16)
    cv = cv_ref[...].astype(BF16)
    for qb, (key0, _) in enumerate(geometry):
        q = q_ref[qb * nq:(qb + 1) * nq, :] * scale
        k = k_ref[key0 * w:key0 * w + nk, :].astype(BF16)
        v = v_ref[key0 * w:key0 * w + nk, :].astype(BF16)
        outs = []
        for head in range(2):
            qh = jnp.where(_head_lanes(q.shape, head), q, 0.0).astype(BF16)
            s_loc = _dot_nt(qh, k) + bias_ref[head, qb]
            s_ctx = _dot_nt(qh, ck)
            m = jnp.maximum(jnp.max(s_loc, axis=-1, keepdims=True), jnp.max(s_ctx, axis=-1, keepdims=True))
            p_loc = jnp.exp(s_loc - m)
            p_ctx = jnp.exp(s_ctx - m)
            denom = jnp.sum(p_loc, axis=-1, keepdims=True) + jnp.sum(p_ctx, axis=-1, keepdims=True)
            outs.append((_dot(p_loc.astype(BF16), v) + _dot(p_ctx.astype(BF16), cv)) / denom)
        o_ref[qb * nq:(qb + 1) * nq, :] = jnp.where(_head_lanes(q.shape, 0), outs[0], outs[1])


def _nbr_attention(proj, row_block0, seq, n_batch, col0, ck, cv, tiles):
    pairs = B_HEADS // 2
    cb = col0 // LANES
    past = ck.shape[2]
    grid_rows = seq // GRID_W
    assert grid_rows % Q_ROWS == 0 and grid_rows >= KEY_ROWS
    geometry = _nbr_geometry(grid_rows)
    nq, nk = Q_ROWS * GRID_W, KEY_ROWS * GRID_W

    def col(section):
        return pl.BlockSpec((seq, LANES), lambda h, b: (row_block0 + b, cb + section * pairs + h))

    return pl.pallas_call(
        functools.partial(_nbr_attn_kernel, scale=1.0 / math.sqrt(LANES // 2), geometry=geometry),
        out_shape=jax.ShapeDtypeStruct((n_batch * seq, pairs * LANES), F32),
        grid=(pairs, n_batch),
        in_specs=[col(0), col(1), col(2),
                  pl.BlockSpec((None, None, past, LANES), lambda h, b: (b, h, 0, 0)),
                  pl.BlockSpec((None, None, past, LANES), lambda h, b: (b, h, 0, 0)),
                  pl.BlockSpec((2,) + tiles.shape[1:], lambda h, b: (h, 0, 0, 0))],
        out_specs=pl.BlockSpec((seq, LANES), lambda h, b: (b, h)),
        scratch_shapes=[pltpu.VMEM((2, len(geometry), nq, nk), F32)],
        compiler_params=_params("arbitrary", "arbitrary"),
        name="neighbourhood_attention",
    )(proj, proj, proj, ck, cv, tiles)


def _head_pairs(t):
    b, h, l, dh = t.shape
    return t.reshape(b, h // 2, 2, l, dh).transpose(0, 1, 3, 2, 4).reshape(b, h // 2, l, 2 * dh)


def kernel(x_prompt, x_sample, c, state_hgrn, cache_k, cache_v, c_ctx, mod_w, mod_b, norm_pre, norm_post,
           ffn_w_in, ffn_w_out, ab_w_in, ab_w_out, a_lb, a_gnorm, b_rpb, cv_w_in, cv_conv, cv_w_out):
    n_ctx, ctx_len, d = x_prompt.shape
    n_lat, lat_len, _ = x_sample.shape
    depth = mod_w.shape[0]
    d_a = a_lb.shape[-1]
    rows = _Rows(n_ctx * ctx_len, ctx_len, n_lat * lat_len, lat_len, ROW_TILE)
    ffn_rows = _Rows(n_ctx * ctx_len, ctx_len, n_lat * lat_len, lat_len, FFN_ROW_TILE)
    fused_rows = _Rows(n_ctx * ctx_len, ctx_len, n_lat * lat_len, lat_len, FUSED_FFN_ROW_TILE)
    assert (n_ctx * ctx_len) % lat_len == 0
    lat_block0 = (n_ctx * ctx_len) // lat_len

    x = (x_prompt.reshape(-1, d), x_sample.reshape(-1, d))

    cond = jnp.concatenate([c_ctx[None, :], c], axis=0)
    cond = jnp.pad(cond, ((0, -cond.shape[0] % 8), (0, 0)))
    mod = _modulation(cond, mod_w, mod_b)
    mod = mod.reshape(depth, cond.shape[0], 1, N_MOD * d)

    ffn_w_out = ffn_w_out.astype(BF16)
    ab_w_out = ab_w_out.astype(BF16)
    cv_w_out = cv_w_out.astype(BF16)

    n_even = a_lb.shape[0]
    layer_states, layer_caches = [], []
    for l in range(depth):
        j = l // 2

        def gains(t, sub):
            return t[l, sub][None, :]

        x = _ffn(x, mod[l], gains(norm_pre, 0), gains(norm_post, 0), (ffn_w_in, (l, 0)), (ffn_w_out, (l, 0)),
                 ffn_rows, 0)
        if l % 2 == 0:
            proj = _inproj(x, mod[l], gains(norm_pre, 1), (ab_w_in, (j,)), rows)
            gn = a_gnorm[j][None, :]
            last = j == n_even - 1
            oa_ctx, new_state = _hgrn(proj, 0, ctx_len, n_ctx, a_lb, gn, j, emit_state=True,
                                      prior=layer_states if last else None)
            oa_lat, = _hgrn(proj, lat_block0, lat_len, n_lat, a_lb, gn, j, s0=state_hgrn)
            ob_ctx, *new_caches = _ctx_attention(proj, n_ctx, ctx_len, 5 * d_a, layer_caches, last)
            layer_states.append(new_state)
            layer_caches.append(new_caches)
            ob_lat = _nbr_attention(proj, lat_block0, lat_len, n_lat, 5 * d_a,
                                    _head_pairs(cache_k[:, j]), _head_pairs(cache_v[:, j]),
                                    _rpb_tiles(b_rpb[j]))
            mixer = ((oa_ctx, oa_lat), (ob_ctx, ob_lat), (ab_w_out, (j,)), gains(norm_post, 1))
        else:
            x = _conv_mixer(x, mod[l], gains(norm_pre, 1), gains(norm_post, 1), (cv_w_in, (j,)), cv_conv[j],
                            (cv_w_out, (j,)), rows)
            mixer = None
        x = _ffn(x, mod[l], gains(norm_pre, 2), gains(norm_post, 2), (ffn_w_in, (l, 1)), (ffn_w_out, (l, 1)),
                 ffn_rows if mixer is None else fused_rows, 2, split_out=(l == depth - 1), mixer=mixer)

    return (x[0].reshape(x_prompt.shape), x[1].reshape(x_sample.shape), new_state, *new_caches)
```

```python
import functools
import math

import jax
import jax.numpy as jnp
import numpy as np
from jax import lax
from jax.experimental import pallas as pl
from jax.experimental.pallas import tpu as pltpu

F32 = jnp.float32
BF16 = jnp.bfloat16

EPS = 1e-6
LB_MAX = 1.0 - 1e-4
MASKED = -1e30

GRID_W = 64
A_HEADS = 4
B_HEADS = 8
NA_ROWS = 8
NA_COLS = 16
CONV_W = 3
N_MOD = 9

VMEM_LIMIT_BYTES = 56 * 1024 * 1024
LANES = 128

ROW_TILE = 512
FFN_ROW_TILE = 512
FFN_SUB_TILES = 2
FUSED_FFN_ROW_TILE = 512
HALO = 16
SCAN_CHUNK = 128
SCAN_GROUP = 16
SCAN_HEADS_PER_STEP = 2
SAFE_EXP2 = 115.0
LOG2_E = math.log2(math.e)
Q_ROWS = 4
KEY_ROWS = 12


def _params(*sem):
    return pltpu.CompilerParams(dimension_semantics=sem, vmem_limit_bytes=VMEM_LIMIT_BYTES)


def _rms(x, g):
    return x * lax.rsqrt(jnp.mean(x * x, axis=-1, keepdims=True) + EPS) * g


def _silu(x):
    return x * jax.nn.sigmoid(x)


def _dot(a, b):
    return jnp.dot(a, b, preferred_element_type=F32)


def _dot_nt(a, b):
    return lax.dot_general(a, b, (((1,), (1,)), ((), ())), preferred_element_type=F32)


def _dot_tn(a, b):
    return lax.dot_general(a, b, (((0,), (0,)), ((), ())), preferred_element_type=F32)


def _resident_spec(w, index):
    lead = len(index)
    zeros = (0,) * (w.ndim - lead)
    return pl.BlockSpec((None,) * lead + tuple(w.shape[lead:]), lambda *_: tuple(index) + zeros,
                        pipeline_mode=pl.Buffered(1))


def _mod_parts(mod, j, d):
    return (mod[:, (3 * j) * d:(3 * j + 1) * d],
            mod[:, (3 * j + 1) * d:(3 * j + 2) * d],
            mod[:, (3 * j + 2) * d:(3 * j + 3) * d])


class _Rows:
    def __init__(self, n_ctx_rows, ctx_len, n_lat_rows, lat_len, tile):
        assert ctx_len % tile == 0 or tile % ctx_len == 0
        assert lat_len % tile == 0 and n_ctx_rows % tile == 0 and n_lat_rows % tile == 0
        self.tile = tile
        self.n_ctx_tiles = n_ctx_rows // tile
        self.n_tiles = (n_ctx_rows + n_lat_rows) // tile
        self.lat_tiles_per_batch = lat_len // tile
        self.ctx_len = ctx_len
        self.lat_len = lat_len

    def mod_row(self, i):
        return jnp.where(i < self.n_ctx_tiles, 0, 1 + (i - self.n_ctx_tiles) // self.lat_tiles_per_batch)

    def group_specs(self, width):
        nc = self.n_ctx_tiles
        return [pl.BlockSpec((self.tile, width), lambda i: (jnp.minimum(i, nc - 1), 0)),
                pl.BlockSpec((self.tile, width), lambda i: (jnp.maximum(i - nc, 0), 0))]

    def mod_spec(self, width):
        return pl.BlockSpec((None, 1, width), lambda i: (self.mod_row(i), 0, 0))


def _mod_kernel(cond_ref, w_ref, b_ref, o_ref):
    s = _silu(cond_ref[...]).astype(BF16)
    o_ref[...] = _dot(s, w_ref[...].astype(BF16)) + b_ref[...]


def _modulation(cond, mod_w, mod_b):
    depth, d, n = mod_w.shape
    rows = cond.shape[0]
    tn = 1536
    assert n % tn == 0
    return pl.pallas_call(
        _mod_kernel,
        out_shape=jax.ShapeDtypeStruct((depth, rows, n), F32),
        grid=(depth, n // tn),
        in_specs=[pl.BlockSpec((rows, d), lambda l, j: (0, 0)),
                  pl.BlockSpec((None, d, tn), lambda l, j: (l, 0, j)),
                  pl.BlockSpec((None, 1, tn), lambda l, j: (l, 0, j))],
        out_specs=pl.BlockSpec((None, rows, tn), lambda l, j: (l, 0, j)),
        compiler_params=_params("parallel", "parallel"),
        name="adaln_mod",
    )(cond, mod_w, mod_b.reshape(depth, 1, n))


def _ffn_kernel(*refs, j, tf, n_sub, n_ctx_tiles, split_in, split_out, fuse_mixer):
    n_x = 2 if split_in else 1
    x_refs = refs[:n_x]
    n_mix = 6 if fuse_mixer else 0
    mix_refs = refs[n_x:n_x + n_mix]
    mod_ref, gpre_ref, gpost_ref, win_ref, wout_ref = refs[n_x + n_mix:n_x + n_mix + 5]
    n_scratch = 3 if fuse_mixer else 2
    o_refs = refs[n_x + n_mix + 5:-n_scratch]
    h_ref, act_ref = refs[-n_scratch:][:2]
    is_ctx = pl.program_id(0) < n_ctx_tiles

    def load_x(rows):
        if split_in:
            return jnp.where(is_ctx, x_refs[0][rows, :], x_refs[1][rows, :])
        return x_refs[0][rows, :]

    tm, d = x_refs[0].shape
    dff = wout_ref.shape[0]
    shift, scale, gate = _mod_parts(mod_ref[...], j, d)
    sub_rows = [slice(sub * (tm // n_sub), (sub + 1) * (tm // n_sub)) for sub in range(n_sub)]
    for rows in sub_rows:
        x_in = load_x(rows)
        if fuse_mixer:
            ac_ref, al_ref, bc_ref, bl_ref, wmix_ref, gmix_ref = mix_refs
            da = ac_ref.shape[1]
            a = jnp.where(is_ctx, ac_ref[rows, :], al_ref[rows, :]).astype(BF16)
            b = jnp.where(is_ctx, bc_ref[rows, :], bl_ref[rows, :]).astype(BF16)
            mixed = _dot(a, wmix_ref[:da, :]) + _dot(b, wmix_ref[da:, :])
            x_in = x_in + _mod_parts(mod_ref[...], 1, d)[2] * _rms(mixed, gmix_ref[...])
            refs[-1][rows, :] = x_in
        h_ref[rows, :] = (_rms(x_in, gpre_ref[...]) * (1.0 + scale) + shift).astype(BF16)
    for c in range(dff // tf):
        w_gate = win_ref[:, c * tf:(c + 1) * tf].astype(BF16)
        w_up = win_ref[:, dff + c * tf:dff + (c + 1) * tf].astype(BF16)
        for rows in sub_rows:
            g = _dot(h_ref[rows, :], w_gate)
            u = _dot(h_ref[rows, :], w_up)
            act_ref[rows, c * tf:(c + 1) * tf] = (_silu(g) * u).astype(BF16)
    for rows in sub_rows:
        out = _dot(act_ref[rows, :], wout_ref[...])
        x_in = refs[-1][rows, :] if fuse_mixer else load_x(rows)
        y = x_in + (0.5 * gate) * _rms(out, gpost_ref[...])
        if split_out:
            @pl.when(is_ctx)
            def _():
                o_refs[0][rows, :] = y

            @pl.when(jnp.logical_not(is_ctx))
            def _():
                o_refs[1][rows, :] = y
        else:
            o_refs[0][rows, :] = y


def _ffn(x, mod_l, g_pre, g_post, w_in, w_out, rows, j, split_out=False, mixer=None):
    split_in = isinstance(x, tuple)
    n_sub = FFN_SUB_TILES if rows.tile >= FFN_ROW_TILE else 1
    mix_args, mix_specs = [], []
    if mixer is not None:
        o_a, o_b, w_mix, g_mix = mixer
        mix_args = [*o_a, *o_b, w_mix[0], g_mix]
        mix_specs = (rows.group_specs(o_a[0].shape[1]) + rows.group_specs(o_b[0].shape[1])
                     + [_resident_spec(*w_mix), pl.BlockSpec((1, g_mix.shape[1]), lambda i: (0, 0))])
    d = x[0].shape[1] if split_in else x.shape[1]
    dff = w_out[0].shape[-2]
    tm = rows.tile
    tf = 256
    assert dff % tf == 0
    out_shape = jax.ShapeDtypeStruct((rows.n_tiles * tm, d), F32)
    out_spec = pl.BlockSpec((tm, d), lambda i: (i, 0))
    if split_out:
        out_shape = (jax.ShapeDtypeStruct((rows.n_ctx_tiles * tm, d), F32),
                     jax.ShapeDtypeStruct(((rows.n_tiles - rows.n_ctx_tiles) * tm, d), F32))
        out_spec = tuple(rows.group_specs(d))
    return pl.pallas_call(
        functools.partial(_ffn_kernel, j=j, tf=tf, n_sub=n_sub, n_ctx_tiles=rows.n_ctx_tiles,
                          split_in=split_in, split_out=split_out, fuse_mixer=mixer is not None),
        out_shape=out_shape,
        grid=(rows.n_tiles,),
        in_specs=(rows.group_specs(d) if split_in else [pl.BlockSpec((tm, d), lambda i: (i, 0))]) + mix_specs + [
                  rows.mod_spec(mod_l.shape[-1]),
                  pl.BlockSpec((1, d), lambda i: (0, 0)),
                  pl.BlockSpec((1, d), lambda i: (0, 0)),
                  _resident_spec(*w_in), _resident_spec(*w_out)],
        out_specs=out_spec,
        scratch_shapes=[pltpu.VMEM((tm, d), BF16), pltpu.VMEM((tm, dff), BF16)]
                       + ([pltpu.VMEM((tm, d), F32)] if mixer is not None else []),
        compiler_params=_params("arbitrary" if split_out else "parallel"),
        name="swiglu_sublayer",
    )(*(x if split_in else (x,)), *mix_args, mod_l, g_pre, g_post, w_in[0], w_out[0])


def _inproj_kernel(x_ref, mod_ref, gpre_ref, w_ref, o_ref, h_ref, *, tn):
    d = x_ref.shape[1]
    shift, scale, _ = _mod_parts(mod_ref[...], 1, d)
    h_ref[...] = (_rms(x_ref[...], gpre_ref[...]) * (1.0 + scale) + shift).astype(BF16)
    for c in range(w_ref.shape[1] // tn):
        o_ref[:, c * tn:(c + 1) * tn] = _dot(h_ref[...], w_ref[:, c * tn:(c + 1) * tn].astype(BF16))


def _inproj(x, mod_l, g_pre, w, rows):
    m, d = x.shape
    n = w[0].shape[-1]
    tm = rows.tile
    tn = 512
    assert n % tn == 0
    return pl.pallas_call(
        functools.partial(_inproj_kernel, tn=tn),
        out_shape=jax.ShapeDtypeStruct((m, n), F32),
        grid=(rows.n_tiles,),
        in_specs=[pl.BlockSpec((tm, d), lambda i: (i, 0)),
                  rows.mod_spec(mod_l.shape[-1]),
                  pl.BlockSpec((1, d), lambda i: (0, 0)),
                  _resident_spec(*w)],
        out_specs=pl.BlockSpec((tm, n), lambda i: (i, 0)),
        scratch_shapes=[pltpu.VMEM((tm, d), BF16)],
        compiler_params=_params("parallel"),
        name="mixer_in_proj",
    )(x, mod_l, g_pre, w[0])


def _conv_kernel(x_ref, xp_ref, xn_ref, mod_ref, gpre_ref, gpost_ref, win_ref, cw_ref, wout_ref, o_ref,
                 h_ref, u_ref, z_ref, *, tc, rows):
    tm, d = x_ref.shape
    i = pl.program_id(0)
    shift, scale, gate = _mod_parts(mod_ref[...], 1, d)
    g_pre = gpre_ref[...]

    def pre(x):
        return (_rms(x, g_pre) * (1.0 + scale) + shift).astype(BF16)

    h_ref[0:HALO, :] = pre(xp_ref[...])
    h_ref[HALO:HALO + tm, :] = pre(x_ref[...])
    h_ref[HALO + tm:, :] = pre(xn_ref[...])

    seq_len = jnp.where(i < rows.n_ctx_tiles, rows.ctx_len, rows.lat_len)
    pos = (i * tm + lax.broadcasted_iota(jnp.int32, (tm, 1), 0)) & (seq_len - 1)
    has_left = pos != 0
    has_right = pos != seq_len - 1

    for c in range(d // tc):
        cols = slice(c * tc, (c + 1) * tc)
        b_gate = _dot(h_ref[HALO:HALO + tm, :], win_ref[:, c * tc:(c + 1) * tc].astype(BF16))
        c_gate = _dot(h_ref[...], win_ref[:, d + c * tc:d + (c + 1) * tc].astype(BF16))
        x_in = _dot(h_ref[...], win_ref[:, 2 * d + c * tc:2 * d + (c + 1) * tc].astype(BF16))
        u_ref[...] = c_gate * x_in
        cw = cw_ref[:, cols]
        y = (cw[0:1, :] * jnp.where(has_left, u_ref[HALO - 1:HALO - 1 + tm, :], 0.0)
             + cw[1:2, :] * u_ref[HALO:HALO + tm, :]
             + cw[2:3, :] * jnp.where(has_right, u_ref[HALO + 1:HALO + 1 + tm, :], 0.0))
        z_ref[:, cols] = (b_gate * y).astype(BF16)
    out = _dot(z_ref[...], wout_ref[...])
    o_ref[...] = x_ref[...] + gate * _rms(out, gpost_ref[...])


def _conv_mixer(x, mod_l, g_pre, g_post, w_in, conv_w, w_out, rows):
    m, d = x.shape
    tm = rows.tile
    tc = 256
    assert rows.ctx_len & (rows.ctx_len - 1) == 0 and rows.lat_len & (rows.lat_len - 1) == 0
    hb = tm // HALO
    last = m // HALO - 1
    return pl.pallas_call(
        functools.partial(_conv_kernel, tc=tc, rows=rows),
        out_shape=jax.ShapeDtypeStruct((m, d), F32),
        grid=(rows.n_tiles,),
        in_specs=[pl.BlockSpec((tm, d), lambda i: (i, 0)),
                  pl.BlockSpec((HALO, d), lambda i: (jnp.maximum(i * hb - 1, 0), 0)),
                  pl.BlockSpec((HALO, d), lambda i: (jnp.minimum((i + 1) * hb, last), 0)),
                  rows.mod_spec(mod_l.shape[-1]),
                  pl.BlockSpec((1, d), lambda i: (0, 0)),
                  pl.BlockSpec((1, d), lambda i: (0, 0)),
                  _resident_spec(*w_in),
                  pl.BlockSpec((CONV_W, d), lambda i: (0, 0)),
                  _resident_spec(*w_out)],
        out_specs=pl.BlockSpec((tm, d), lambda i: (i, 0)),
        scratch_shapes=[pltpu.VMEM((tm + 2 * HALO, d), BF16),
                        pltpu.VMEM((tm + 2 * HALO, tc), F32),
                        pltpu.VMEM((tm, d), BF16)],
        compiler_params=_params("parallel"),
        name="conv_mixer",
    )(x, x, x, mod_l, g_pre, g_post, w_in[0], conv_w, w_out[0])


def _scan_constants(c):
    t, s = np.meshgrid(np.arange(c), np.arange(c), indexing="ij")
    x = t ^ s
    n_levels = int(math.log2(c // SCAN_GROUP))
    code = np.full((c, c), n_levels, np.int32)
    for lvl in range(n_levels - 1, 0, -1):
        code = np.where(x >= (c >> lvl), lvl, code)
    code = np.where(x < SCAN_GROUP, n_levels + 1, code)
    visited = np.stack([s <= t, s >= t])
    codes = np.where(visited, code[None], 0).astype(np.int32)
    return jnp.asarray(codes), jnp.asarray(visited, BF16), n_levels


def _gate_chunk(z, lb, cum, k_ref, c_ref, base):
    c_len, dk = z.shape
    e_abs = jnp.exp(-jnp.abs(z))
    inv = 1.0 / (1.0 + e_abs)
    sig = jnp.where(z >= 0, inv, e_abs * inv)
    sig_neg = jnp.where(z >= 0, e_abs * inv, inv)
    f = lb + (1.0 - lb) * sig
    log_sig = jnp.minimum(z, 0.0) + jnp.log(inv)
    log_f = jnp.maximum(jnp.log1p(-lb) + log_sig, jnp.log(jnp.maximum(f, 1e-30))) * LOG2_E
    k_ref[pl.ds(base, c_len), :] = (1.0 - lb) * sig_neg

    hi = log_f.astype(BF16)
    rest = log_f - hi.astype(F32)
    mid = rest.astype(BF16)
    lo = (rest - mid.astype(F32)).astype(BF16)
    sums = _dot(cum, jnp.concatenate([hi, mid, lo], axis=1))
    c = sums[:, :dk] + sums[:, dk:2 * dk] + sums[:, 2 * dk:]
    c_ref[pl.ds(base, c_len), :] = c

    span = jnp.zeros((1, dk), F32)
    for g in range(0, c_len, SCAN_GROUP):
        ends = c_ref[pl.ds(base + g, 1), :] - c_ref[pl.ds(base + g + SCAN_GROUP - 1, 1), :]
        span = jnp.maximum(span, jnp.abs(ends))
    return span


def _scan_chunk(q, k, c, v, c_row, st_ref, kt_ref, code, n_levels, forward, factorised):
    c_len, dk = q.shape
    q16 = q.astype(BF16)
    k16 = k.astype(BF16)

    def reference_rows(block, offset):
        parts = [jnp.broadcast_to(c_row(n * block + offset), (block, dk)) for n in range(c_len // block)]
        return parts[0] if len(parts) == 1 else jnp.concatenate(parts, axis=0)

    def scores(slot, q_scaled, k_scaled):
        kt_ref[slot] = k_scaled.T
        return _dot(q_scaled, kt_ref[slot])

    att = jnp.zeros((c_len, c_len), F32)
    for lvl in range(1, n_levels + 1):
        half = c_len >> lvl
        e = jnp.exp2(-jnp.abs(c - reference_rows(2 * half, half))).astype(BF16)
        att = jnp.where(code == lvl, scores(lvl, q16 * e, k16 * e), att)

    if factorised:
        dq = c - reference_rows(SCAN_GROUP, 0 if forward else SCAN_GROUP - 1)
        same = scores(0, q16 * jnp.exp2(dq).astype(BF16), k16 * jnp.exp2(-dq).astype(BF16))
    else:
        row = lax.broadcasted_iota(jnp.int32, (c_len, dk), 0)
        t = lax.broadcasted_iota(jnp.int32, (c_len, c_len), 0)
        s = lax.broadcasted_iota(jnp.int32, (c_len, c_len), 1)
        pos = row & (SCAN_GROUP - 1)
        same = jnp.zeros((c_len, c_len), F32)
        for dist in range(SCAN_GROUP):
            if dist == 0:
                k_s, c_s = k, c
            else:
                shift = dist if forward else c_len - dist
                k_s, c_s = pltpu.roll(k, shift, 0), pltpu.roll(c, shift, 0)
            valid = (pos >= dist) if forward else (pos + dist < SCAN_GROUP)
            term = q * k_s * jnp.exp2(jnp.where(valid, c - c_s, 0.0))
            a = jnp.sum(jnp.where(valid, term, 0.0), axis=-1, keepdims=True)
            same = jnp.where(s == (t - dist if forward else t + dist), a, same)
    att = jnp.where(code == n_levels + 1, same, att)

    st = st_ref[...]
    v16 = v.astype(BF16)
    o = _dot(att.astype(BF16), v16) + _dot(q16 * jnp.exp2(c).astype(BF16), st.astype(BF16))
    c_end = c_row(c_len - 1 if forward else 0)
    row_decay = jnp.broadcast_to(jnp.exp2(c_end), (dk, dk)).T
    st_ref[...] = st * row_decay + _dot_tn(k16 * jnp.exp2(c_end - c).astype(BF16), v16)
    return o


def _hgrn_kernel(*refs, layer_j, chunk, n_levels, heads, has_s0, n_prior, emit_state):
    q_ref, zf_ref, zb_ref, v_ref, g_ref, alb_ref, gn_ref, code_ref, sums_ref = refs[:9]
    n_in = 9 + int(has_s0) + (n_prior or 0)
    o_ref = refs[n_in]
    s_ref, o_dir_ref, k_ref, c_ref, kt_ref = refs[-5:]
    seq = q_ref.shape[0]
    dk = q_ref.shape[1] // heads
    n_chunks = seq // chunk
    z_refs = (zf_ref, zb_ref)

    def chunk_base(n):
        return pl.multiple_of(n * chunk, chunk)

    def lanes(head):
        return slice(head * dk, (head + 1) * dk)

    a = alb_ref[...]
    e = jnp.exp(a - jnp.max(a, axis=0, keepdims=True))
    p = e / jnp.sum(e, axis=0, keepdims=True)
    cum = p[0]
    for n in range(1, layer_j + 1):
        cum = cum + p[n]
    lb = jnp.clip(cum - p[0], 0.0, LB_MAX)

    def gates(n, span):
        base = chunk_base(n)
        rows = pl.ds(base, chunk)
        for head in range(heads):
            for direction in range(2):
                span = jnp.maximum(span, _gate_chunk(
                    z_refs[direction][rows, lanes(head)], lb[direction:direction + 1, lanes(head)],
                    sums_ref[direction], k_ref.at[direction, head], c_ref.at[direction, head], base))
        return span

    unroll = 2 if n_chunks % 2 == 0 else 1
    span = lax.fori_loop(0, n_chunks, gates, jnp.zeros((1, dk), F32), unroll=unroll)
    safe = jnp.max(span) <= SAFE_EXP2

    for head in range(heads):
        for direction in range(2):
            s_ref[direction, head] = refs[9][direction, head] if has_s0 else jnp.zeros((dk, dk), F32)

    def scan(factorised):
        def body(n, carry):
            for head in range(heads):
                for direction in range(2):
                    base = chunk_base(n if direction == 0 else n_chunks - 1 - n)
                    rows = pl.ds(base, chunk)
                    c_dir = c_ref.at[direction, head]
                    o_dir_ref[direction, head, rows, :] = _scan_chunk(
                        q_ref[rows, lanes(head)], k_ref[direction, head, rows, :], c_dir[rows, :],
                        v_ref[rows, lanes(head)], lambda r, c_dir=c_dir, base=base: c_dir[pl.ds(base + r, 1), :],
                        s_ref.at[direction, head], kt_ref.at[direction, head], code_ref[direction], n_levels,
                        direction == 0, factorised)
            return carry

        lax.fori_loop(0, n_chunks, body, 0, unroll=unroll if factorised else 1)

    lax.cond(safe, lambda: scan(True), lambda: scan(False))

    for head in range(heads):
        o = o_dir_ref[0, head] + o_dir_ref[1, head]
        o_ref[:, lanes(head)] = _rms(o, gn_ref[...]) * _silu(g_ref[:, lanes(head)])
        if emit_state:
            so_ref = refs[n_in + 1]
            if n_prior is not None:
                for n in range(n_prior):
                    so_ref[n, :, head] = refs[9 + int(has_s0) + n][:, head]
                so_ref = so_ref.at[n_prior]
            for direction in range(2):
                so_ref[direction, head] = s_ref[direction, head]


def _state_spec(layer_j, heads):
    return pl.BlockSpec((None, None, 2, heads, LANES, LANES), lambda b, h: (b, layer_j, 0, h, 0, 0))


def _hgrn(proj, row_block0, seq, n_batch, a_lb, gnorm, layer_j, s0=None, emit_state=False, prior=None):
    dk = LANES
    heads = A_HEADS if seq <= 2 * SCAN_CHUNK else SCAN_HEADS_PER_STEP
    groups = A_HEADS // heads
    chunk = min(SCAN_CHUNK, seq)
    codes, sums, n_levels = _scan_constants(chunk)
    n_layers = a_lb.shape[0]

    def col(section):
        return pl.BlockSpec((seq, heads * dk), lambda b, h: (row_block0 + b, section * groups + h))

    def const(a):
        return pl.BlockSpec(a.shape, lambda b, h: (0,) * a.ndim)

    layer_state_spec = pl.BlockSpec((None, 2, heads, dk, dk), lambda b, h: (b, 0, h, 0, 0))
    extra, extra_specs = [], []
    if s0 is not None:
        extra.append(s0)
        extra_specs.append(_state_spec(layer_j, heads))
    if prior is not None:
        extra += list(prior)
        extra_specs += [layer_state_spec] * len(prior)
    out_shape = [jax.ShapeDtypeStruct((n_batch * seq, A_HEADS * dk), F32)]
    out_specs = [pl.BlockSpec((seq, heads * dk), lambda b, h: (b, h))]
    if emit_state and prior is not None:
        stacked = len(prior) + 1
        out_shape.append(jax.ShapeDtypeStruct((n_batch, stacked, 2, A_HEADS, dk, dk), F32))
        out_specs.append(pl.BlockSpec((None, stacked, 2, heads, dk, dk), lambda b, h: (b, 0, 0, h, 0, 0)))
    elif emit_state:
        out_shape.append(jax.ShapeDtypeStruct((n_batch, 2, A_HEADS, dk, dk), F32))
        out_specs.append(layer_state_spec)
    per_head = (2, heads, seq, dk)
    return pl.pallas_call(
        functools.partial(_hgrn_kernel, layer_j=layer_j, chunk=chunk, n_levels=n_levels, heads=heads,
                          has_s0=s0 is not None, n_prior=None if prior is None else len(prior),
                          emit_state=emit_state),
        out_shape=tuple(out_shape),
        grid=(n_batch, groups),
        in_specs=[col(0), col(1), col(2), col(3), col(4),
                  pl.BlockSpec((n_layers, 2, heads * dk), lambda b, h: (0, 0, h)),
                  pl.BlockSpec((1, dk), lambda b, h: (0, 0)), const(codes), const(sums)] + extra_specs,
        out_specs=tuple(out_specs),
        scratch_shapes=[pltpu.VMEM((2, heads, dk, dk), F32), pltpu.VMEM(per_head, F32),
                        pltpu.VMEM(per_head, F32), pltpu.VMEM(per_head, F32),
                        pltpu.VMEM((2, heads, n_levels + 1, dk, chunk), BF16)],
        compiler_params=_params("parallel", "parallel"),
        name="hgrn2_scan",
    )(proj, proj, proj, proj, proj, a_lb, gnorm, codes, sums, *extra)


def _head_lanes(shape, head):
    lane = lax.broadcasted_iota(jnp.int32, shape, len(shape) - 1)
    half = shape[-1] // 2
    return (lane < half) if head == 0 else (lane >= half)


def _ctx_attn_kernel(*refs, scale, n_prior):
    q_ref, k_ref, v_ref = refs[:3]
    o_ref, ko_ref, vo_ref = refs[-3:]
    dh = LANES // 2
    if n_prior is not None:
        for n in range(n_prior):
            ko_ref[n] = refs[3 + n][...]
            vo_ref[n] = refs[3 + n_prior + n][...]
        ko_ref, vo_ref = ko_ref.at[n_prior], vo_ref.at[n_prior]
    for pair in range(q_ref.shape[1] // LANES):
        lanes = slice(pair * LANES, (pair + 1) * LANES)
        k32 = k_ref[:, lanes]
        v32 = v_ref[:, lanes]
        k = k32.astype(BF16)
        v = v32.astype(BF16)
        q = q_ref[:, lanes]
        outs = []
        for head in range(2):
            ko_ref[2 * pair + head] = k32[:, head * dh:(head + 1) * dh]
            vo_ref[2 * pair + head] = v32[:, head * dh:(head + 1) * dh]
            qh = jnp.where(_head_lanes(q.shape, head), q, 0.0).astype(BF16)
            s = _dot_nt(qh, k) * scale
            p = jnp.exp(s - jnp.max(s, axis=-1, keepdims=True))
            outs.append(_dot(p.astype(BF16), v) / jnp.sum(p, axis=-1, keepdims=True))
        o_ref[:, lanes] = jnp.where(_head_lanes(q.shape, 0), outs[0], outs[1])


def _ctx_attention(proj, n_batch, seq, col0, prior, last):
    width = B_HEADS // 2 * LANES
    dh = LANES // 2
    cb = col0 // width
    assert col0 % width == 0
    layer_spec = pl.BlockSpec((None, B_HEADS, seq, dh), lambda b: (b, 0, 0, 0))
    if last:
        n_layers = len(prior) + 1
        cache = jax.ShapeDtypeStruct((n_batch, n_layers, B_HEADS, seq, dh), F32)
        cache_spec = pl.BlockSpec((None, n_layers, B_HEADS, seq, dh), lambda b: (b, 0, 0, 0, 0))
        earlier = [kv[0] for kv in prior] + [kv[1] for kv in prior]
    else:
        cache = jax.ShapeDtypeStruct((n_batch, B_HEADS, seq, dh), F32)
        cache_spec = layer_spec
        earlier = []

    def col(section):
        return pl.BlockSpec((seq, width), lambda b: (b, cb + section))

    return pl.pallas_call(
        functools.partial(_ctx_attn_kernel, scale=1.0 / math.sqrt(dh), n_prior=len(prior) if last else None),
        out_shape=(jax.ShapeDtypeStruct((n_batch * seq, width), F32), cache, cache),
        grid=(n_batch,),
        in_specs=[col(0), col(1), col(2)] + [layer_spec] * len(earlier),
        out_specs=(pl.BlockSpec((seq, width), lambda b: (b, 0)), cache_spec, cache_spec),
        compiler_params=_params("parallel"),
        name="context_attention",
    )(proj, proj, proj, *earlier)


def _rpb_tiles_kernel(rpb_ref, o_ref):
    h = pl.program_id(0)
    qc = lax.broadcasted_iota(jnp.int32, (GRID_W, GRID_W), 0)
    kc = lax.broadcasted_iota(jnp.int32, (GRID_W, GRID_W), 1)
    c0 = jnp.clip(qc - NA_COLS // 2, 0, GRID_W - NA_COLS)
    col_ok = (kc >= c0) & (kc < c0 + NA_COLS)
    dc = jnp.clip(kc - qc, -(NA_COLS - 1), NA_COLS - 1) + NA_COLS - 1
    for dr in range(2 * NA_ROWS - 1):
        t = jnp.zeros((GRID_W, GRID_W), F32)
        for i in range(2 * NA_COLS - 1):
            t = jnp.where(dc == i, rpb_ref[h, dr, i], t)
        o_ref[dr] = jnp.where(col_ok, t, MASKED)


def _rpb_tiles(rpb):
    heads, n_dr, n_dc = rpb.shape
    return pl.pallas_call(
        _rpb_tiles_kernel,
        out_shape=jax.ShapeDtypeStruct((heads, n_dr, GRID_W, GRID_W), F32),
        grid=(heads,),
        in_specs=[pl.BlockSpec(memory_space=pltpu.SMEM)],
        out_specs=pl.BlockSpec((None, n_dr, GRID_W, GRID_W), lambda h: (h, 0, 0, 0)),
        compiler_params=_params("parallel"),
        name="rpb_tiles",
    )(rpb)


def _nbr_geometry(grid_rows):
    win = min(NA_ROWS, grid_rows)
    blocks = []
    for qb in range(grid_rows // Q_ROWS):
        r0 = qb * Q_ROWS
        key0 = min(max(r0 - win // 2, 0), grid_rows - KEY_ROWS)
        tiles = []
        for i in range(Q_ROWS):
            r = r0 + i
            start = min(max(r - win // 2, 0), grid_rows - win)
            assert key0 <= start and start + win <= key0 + KEY_ROWS
            tiles.append([(key0 + jr - r + NA_ROWS - 1) if start <= key0 + jr < start + win else None
                          for jr in range(KEY_ROWS)])
        blocks.append((key0, tiles))
    return blocks


def _nbr_attn_kernel(q_ref, k_ref, v_ref, ck_ref, cv_ref, tiles_ref, o_ref, bias_ref, *, scale, geometry):
    w = GRID_W
    nq = Q_ROWS * w
    nk = KEY_ROWS * w

    @pl.when(pl.program_id(1) == 0)
    def _():
        masked = jnp.full((w, w), MASKED, F32)
        for head in range(2):
            for qb, (_, tiles) in enumerate(geometry):
                for i, row_tiles in enumerate(tiles):
                    for jr, dr in enumerate(row_tiles):
                        tile = masked if dr is None else tiles_ref[head, dr]
                        bias_ref[head, qb, i * w:(i + 1) * w, jr * w:(jr + 1) * w] = tile

    ck = ck_ref[...].astype(BF16)
    cv = cv_ref[...].astype(BF16)
    for qb, (key0, _) in enumerate(geometry):
        q = q_ref[qb * nq:(qb + 1) * nq, :] * scale
        k = k_ref[key0 * w:key0 * w + nk, :].astype(BF16)
        v = v_ref[key0 * w:key0 * w + nk, :].astype(BF16)
        outs = []
        for head in range(2):
            qh = jnp.where(_head_lanes(q.shape, head), q, 0.0).astype(BF16)
            s_loc = _dot_nt(qh, k) + bias_ref[head, qb]
            s_ctx = _dot_nt(qh, ck)
            m = jnp.maximum(jnp.max(s_loc, axis=-1, keepdims=True), jnp.max(s_ctx, axis=-1, keepdims=True))
            p_loc = jnp.exp(s_loc - m)
            p_ctx = jnp.exp(s_ctx - m)
            denom = jnp.sum(p_loc, axis=-1, keepdims=True) + jnp.sum(p_ctx, axis=-1, keepdims=True)
            outs.append((_dot(p_loc.astype(BF16), v) + _dot(p_ctx.astype(BF16), cv)) / denom)
        o_ref[qb * nq:(qb + 1) * nq, :] = jnp.where(_head_lanes(q.shape, 0), outs[0], outs[1])


def _nbr_attention(proj, row_block0, seq, n_batch, col0, ck, cv, tiles):
    pairs = B_HEADS // 2
    cb = col0 // LANES
    past = ck.shape[2]
    grid_rows = seq // GRID_W
    assert grid_rows % Q_ROWS == 0 and grid_rows >= KEY_ROWS
    geometry = _nbr_geometry(grid_rows)
    nq, nk = Q_ROWS * GRID_W, KEY_ROWS * GRID_W

    def col(section):
        return pl.BlockSpec((seq, LANES), lambda h, b: (row_block0 + b, cb + section * pairs + h))

    return pl.pallas_call(
        functools.partial(_nbr_attn_kernel, scale=1.0 / math.sqrt(LANES // 2), geometry=geometry),
        out_shape=jax.ShapeDtypeStruct((n_batch * seq, pairs * LANES), F32),
        grid=(pairs, n_batch),
        in_specs=[col(0), col(1), col(2),
                  pl.BlockSpec((None, None, past, LANES), lambda h, b: (b, h, 0, 0)),
                  pl.BlockSpec((None, None, past, LANES), lambda h, b: (b, h, 0, 0)),
                  pl.BlockSpec((2,) + tiles.shape[1:], lambda h, b: (h, 0, 0, 0))],
        out_specs=pl.BlockSpec((seq, LANES), lambda h, b: (b, h)),
        scratch_shapes=[pltpu.VMEM((2, len(geometry), nq, nk), F32)],
        compiler_params=_params("arbitrary", "arbitrary"),
        name="neighbourhood_attention",
    )(proj, proj, proj, ck, cv, tiles)


def _head_pairs(t):
    b, h, l, dh = t.shape
    return t.reshape(b, h // 2, 2, l, dh).transpose(0, 1, 3, 2, 4).reshape(b, h // 2, l, 2 * dh)


def kernel(x_prompt, x_sample, c, state_hgrn, cache_k, cache_v, c_ctx, mod_w, mod_b, norm_pre, norm_post,
           ffn_w_in, ffn_w_out, ab_w_in, ab_w_out, a_lb, a_gnorm, b_rpb, cv_w_in, cv_conv, cv_w_out):
    n_ctx, ctx_len, d = x_prompt.shape
    n_lat, lat_len, _ = x_sample.shape
    depth = mod_w.shape[0]
    d_a = a_lb.shape[-1]
    rows = _Rows(n_ctx * ctx_len, ctx_len, n_lat * lat_len, lat_len, ROW_TILE)
    ffn_rows = _Rows(n_ctx * ctx_len, ctx_len, n_lat * lat_len, lat_len, FFN_ROW_TILE)
    fused_rows = _Rows(n_ctx * ctx_len, ctx_len, n_lat * lat_len, lat_len, FUSED_FFN_ROW_TILE)
    assert (n_ctx * ctx_len) % lat_len == 0
    lat_block0 = (n_ctx * ctx_len) // lat_len

    x = (x_prompt.reshape(-1, d), x_sample.reshape(-1, d))

    cond = jnp.concatenate([c_ctx[None, :], c], axis=0)
    cond = jnp.pad(cond, ((0, -cond.shape[0] % 8), (0, 0)))
    mod = _modulation(cond, mod_w, mod_b)
    mod = mod.reshape(depth, cond.shape[0], 1, N_MOD * d)

    ffn_w_out = ffn_w_out.astype(BF16)
    ab_w_out = ab_w_out.astype(BF16)
    cv_w_out = cv_w_out.astype(BF16)

    n_even = a_lb.shape[0]
    layer_states, layer_caches = [], []
    for l in range(depth):
        j = l // 2

        def gains(t, sub):
            return t[l, sub][None, :]

        x = _ffn(x, mod[l], gains(norm_pre, 0), gains(norm_post, 0), (ffn_w_in, (l, 0)), (ffn_w_out, (l, 0)),
                 ffn_rows, 0)
        if l % 2 == 0:
            proj = _inproj(x, mod[l], gains(norm_pre, 1), (ab_w_in, (j,)), rows)
            gn = a_gnorm[j][None, :]
            last = j == n_even - 1
            oa_ctx, new_state = _hgrn(proj, 0, ctx_len, n_ctx, a_lb, gn, j, emit_state=True,
                                      prior=layer_states if last else None)
            oa_lat, = _hgrn(proj, lat_block0, lat_len, n_lat, a_lb, gn, j, s0=state_hgrn)
            ob_ctx, *new_caches = _ctx_attention(proj, n_ctx, ctx_len, 5 * d_a, layer_caches, last)
            layer_states.append(new_state)
            layer_caches.append(new_caches)
            ob_lat = _nbr_attention(proj, lat_block0, lat_len, n_lat, 5 * d_a,
                                    _head_pairs(cache_k[:, j]), _head_pairs(cache_v[:, j]),
                                    _rpb_tiles(b_rpb[j]))
            mixer = ((oa_ctx, oa_lat), (ob_ctx, ob_lat), (ab_w_out, (j,)), gains(norm_post, 1))
        else:
            x = _conv_mixer(x, mod[l], gains(norm_pre, 1), gains(norm_post, 1), (cv_w_in, (j,)), cv_conv[j],
                            (cv_w_out, (j,)), rows)
            mixer = None
        x = _ffn(x, mod[l], gains(norm_pre, 2), gains(norm_post, 2), (ffn_w_in, (l, 1)), (ffn_w_out, (l, 1)),
                 ffn_rows if mixer is None else fused_rows, 2, split_out=(l == depth - 1), mixer=mixer)

    return (x[0].reshape(x_prompt.shape), x[1].reshape(x_sample.shape), new_state, *new_caches)
```

```python
import functools
import math

import jax
import jax.numpy as jnp
import numpy as np
from jax import lax
from jax.experimental import pallas as pl
from jax.experimental.pallas import tpu as pltpu

F32 = jnp.float32
BF16 = jnp.bfloat16

EPS = 1e-6
LB_MAX = 1.0 - 1e-4
MASKED = -1e30

GRID_W = 64
A_HEADS = 4
B_HEADS = 8
NA_ROWS = 8
NA_COLS = 16
CONV_W = 3
N_MOD = 9

VMEM_LIMIT_BYTES = 56 * 1024 * 1024
LANES = 128

ROW_TILE = 512
FFN_ROW_TILE = 512
FFN_SUB_TILES = 2
HALO = 16
SCAN_CHUNK = 128
SCAN_GROUP = 16
SCAN_HEADS_PER_STEP = 2
SAFE_EXP2 = 115.0
LOG2_E = math.log2(math.e)
Q_ROWS = 4
KEY_ROWS = 12


def _params(*sem):
    return pltpu.CompilerParams(dimension_semantics=sem, vmem_limit_bytes=VMEM_LIMIT_BYTES)


def _rms(x, g):
    return x * lax.rsqrt(jnp.mean(x * x, axis=-1, keepdims=True) + EPS) * g


def _silu(x):
    return x * jax.nn.sigmoid(x)


def _dot(a, b):
    return jnp.dot(a, b, preferred_element_type=F32)


def _dot_nt(a, b):
    return lax.dot_general(a, b, (((1,), (1,)), ((), ())), preferred_element_type=F32)


def _dot_tn(a, b):
    return lax.dot_general(a, b, (((0,), (0,)), ((), ())), preferred_element_type=F32)


def _resident_spec(w, index):
    lead = len(index)
    zeros = (0,) * (w.ndim - lead)
    return pl.BlockSpec((None,) * lead + tuple(w.shape[lead:]), lambda *_: tuple(index) + zeros,
                        pipeline_mode=pl.Buffered(1))


def _mod_parts(mod, j, d):
    return (mod[:, (3 * j) * d:(3 * j + 1) * d],
            mod[:, (3 * j + 1) * d:(3 * j + 2) * d],
            mod[:, (3 * j + 2) * d:(3 * j + 3) * d])


class _Rows:
    def __init__(self, n_ctx_rows, ctx_len, n_lat_rows, lat_len, tile):
        assert ctx_len % tile == 0 or tile % ctx_len == 0
        assert lat_len % tile == 0 and n_ctx_rows % tile == 0 and n_lat_rows % tile == 0
        self.tile = tile
        self.n_ctx_tiles = n_ctx_rows // tile
        self.n_tiles = (n_ctx_rows + n_lat_rows) // tile
        self.lat_tiles_per_batch = lat_len // tile
        self.ctx_len = ctx_len
        self.lat_len = lat_len

    def mod_row(self, i):
        return jnp.where(i < self.n_ctx_tiles, 0, 1 + (i - self.n_ctx_tiles) // self.lat_tiles_per_batch)

    def group_specs(self, width):
        nc = self.n_ctx_tiles
        return [pl.BlockSpec((self.tile, width), lambda i: (jnp.minimum(i, nc - 1), 0)),
                pl.BlockSpec((self.tile, width), lambda i: (jnp.maximum(i - nc, 0), 0))]

    def mod_spec(self, width):
        return pl.BlockSpec((None, 1, width), lambda i: (self.mod_row(i), 0, 0))


def _mod_kernel(cond_ref, w_ref, b_ref, o_ref):
    s = _silu(cond_ref[...]).astype(BF16)
    o_ref[...] = _dot(s, w_ref[...].astype(BF16)) + b_ref[...]


def _modulation(cond, mod_w, mod_b):
    depth, d, n = mod_w.shape
    rows = cond.shape[0]
    tn = 2304
    assert n % tn == 0
    return pl.pallas_call(
        _mod_kernel,
        out_shape=jax.ShapeDtypeStruct((depth, rows, n), F32),
        grid=(depth, n // tn),
        in_specs=[pl.BlockSpec((rows, d), lambda l, j: (0, 0)),
                  pl.BlockSpec((None, d, tn), lambda l, j: (l, 0, j)),
                  pl.BlockSpec((None, 1, tn), lambda l, j: (l, 0, j))],
        out_specs=pl.BlockSpec((None, rows, tn), lambda l, j: (l, 0, j)),
        compiler_params=_params("parallel", "parallel"),
        name="adaln_mod",
    )(cond, mod_w, mod_b.reshape(depth, 1, n))


def _ffn_kernel(*refs, j, tf, n_sub, n_ctx_tiles, split_in, split_out, fuse_mixer):
    n_x = 2 if split_in else 1
    x_refs = refs[:n_x]
    n_mix = 6 if fuse_mixer else 0
    mix_refs = refs[n_x:n_x + n_mix]
    mod_ref, gpre_ref, gpost_ref, win_ref, wout_ref = refs[n_x + n_mix:n_x + n_mix + 5]
    n_scratch = 3 if fuse_mixer else 2
    o_refs = refs[n_x + n_mix + 5:-n_scratch]
    h_ref, act_ref = refs[-n_scratch:][:2]
    is_ctx = pl.program_id(0) < n_ctx_tiles

    def load_x(rows):
        if split_in:
            return jnp.where(is_ctx, x_refs[0][rows, :], x_refs[1][rows, :])
        return x_refs[0][rows, :]

    tm, d = x_refs[0].shape
    dff = wout_ref.shape[0]
    shift, scale, gate = _mod_parts(mod_ref[...], j, d)
    sub_rows = [slice(sub * (tm // n_sub), (sub + 1) * (tm // n_sub)) for sub in range(n_sub)]
    for rows in sub_rows:
        x_in = load_x(rows)
        if fuse_mixer:
            ac_ref, al_ref, bc_ref, bl_ref, wmix_ref, gmix_ref = mix_refs
            da = ac_ref.shape[1]
            a = jnp.where(is_ctx, ac_ref[rows, :], al_ref[rows, :]).astype(BF16)
            b = jnp.where(is_ctx, bc_ref[rows, :], bl_ref[rows, :]).astype(BF16)
            mixed = _dot(a, wmix_ref[:da, :]) + _dot(b, wmix_ref[da:, :])
            x_in = x_in + _mod_parts(mod_ref[...], 1, d)[2] * _rms(mixed, gmix_ref[...])
            refs[-1][rows, :] = x_in
        h_ref[rows, :] = (_rms(x_in, gpre_ref[...]) * (1.0 + scale) + shift).astype(BF16)
    for c in range(dff // tf):
        w_gate = win_ref[:, c * tf:(c + 1) * tf].astype(BF16)
        w_up = win_ref[:, dff + c * tf:dff + (c + 1) * tf].astype(BF16)
        for rows in sub_rows:
            g = _dot(h_ref[rows, :], w_gate)
            u = _dot(h_ref[rows, :], w_up)
            act_ref[rows, c * tf:(c + 1) * tf] = (_silu(g) * u).astype(BF16)
    for rows in sub_rows:
        out = _dot(act_ref[rows, :], wout_ref[...])
        x_in = refs[-1][rows, :] if fuse_mixer else load_x(rows)
        y = x_in + (0.5 * gate) * _rms(out, gpost_ref[...])
        if split_out:
            @pl.when(is_ctx)
            def _():
                o_refs[0][rows, :] = y

            @pl.when(jnp.logical_not(is_ctx))
            def _():
                o_refs[1][rows, :] = y
        else:
            o_refs[0][rows, :] = y


def _ffn(x, mod_l, g_pre, g_post, w_in, w_out, rows, j, split_out=False, mixer=None):
    split_in = isinstance(x, tuple)
    n_sub = FFN_SUB_TILES if rows.tile >= FFN_ROW_TILE else 1
    mix_args, mix_specs = [], []
    if mixer is not None:
        o_a, o_b, w_mix, g_mix = mixer
        mix_args = [*o_a, *o_b, w_mix[0], g_mix]
        mix_specs = (rows.group_specs(o_a[0].shape[1]) + rows.group_specs(o_b[0].shape[1])
                     + [_resident_spec(*w_mix), pl.BlockSpec((1, g_mix.shape[1]), lambda i: (0, 0))])
    d = x[0].shape[1] if split_in else x.shape[1]
    dff = w_out[0].shape[-2]
    tm = rows.tile
    tf = 256
    assert dff % tf == 0
    out_shape = jax.ShapeDtypeStruct((rows.n_tiles * tm, d), F32)
    out_spec = pl.BlockSpec((tm, d), lambda i: (i, 0))
    if split_out:
        out_shape = (jax.ShapeDtypeStruct((rows.n_ctx_tiles * tm, d), F32),
                     jax.ShapeDtypeStruct(((rows.n_tiles - rows.n_ctx_tiles) * tm, d), F32))
        out_spec = tuple(rows.group_specs(d))
    return pl.pallas_call(
        functools.partial(_ffn_kernel, j=j, tf=tf, n_sub=n_sub, n_ctx_tiles=rows.n_ctx_tiles,
                          split_in=split_in, split_out=split_out, fuse_mixer=mixer is not None),
        out_shape=out_shape,
        grid=(rows.n_tiles,),
        in_specs=(rows.group_specs(d) if split_in else [pl.BlockSpec((tm, d), lambda i: (i, 0))]) + mix_specs + [
                  rows.mod_spec(mod_l.shape[-1]),
                  pl.BlockSpec((1, d), lambda i: (0, 0)),
                  pl.BlockSpec((1, d), lambda i: (0, 0)),
                  _resident_spec(*w_in), _resident_spec(*w_out)],
        out_specs=out_spec,
        scratch_shapes=[pltpu.VMEM((tm, d), BF16), pltpu.VMEM((tm, dff), BF16)]
                       + ([pltpu.VMEM((tm, d), F32)] if mixer is not None else []),
        compiler_params=_params("arbitrary" if split_out else "parallel"),
        name="swiglu_sublayer",
    )(*(x if split_in else (x,)), *mix_args, mod_l, g_pre, g_post, w_in[0], w_out[0])


def _inproj_kernel(x_ref, mod_ref, gpre_ref, w_ref, o_ref, h_ref, *, tn):
    d = x_ref.shape[1]
    shift, scale, _ = _mod_parts(mod_ref[...], 1, d)
    h_ref[...] = (_rms(x_ref[...], gpre_ref[...]) * (1.0 + scale) + shift).astype(BF16)
    for c in range(w_ref.shape[1] // tn):
        o_ref[:, c * tn:(c + 1) * tn] = _dot(h_ref[...], w_ref[:, c * tn:(c + 1) * tn].astype(BF16))


def _inproj(x, mod_l, g_pre, w, rows):
    m, d = x.shape
    n = w[0].shape[-1]
    tm = rows.tile
    tn = 512
    assert n % tn == 0
    return pl.pallas_call(
        functools.partial(_inproj_kernel, tn=tn),
        out_shape=jax.ShapeDtypeStruct((m, n), F32),
        grid=(rows.n_tiles,),
        in_specs=[pl.BlockSpec((tm, d), lambda i: (i, 0)),
                  rows.mod_spec(mod_l.shape[-1]),
                  pl.BlockSpec((1, d), lambda i: (0, 0)),
                  _resident_spec(*w)],
        out_specs=pl.BlockSpec((tm, n), lambda i: (i, 0)),
        scratch_shapes=[pltpu.VMEM((tm, d), BF16)],
        compiler_params=_params("parallel"),
        name="mixer_in_proj",
    )(x, mod_l, g_pre, w[0])


def _conv_kernel(x_ref, xp_ref, xn_ref, mod_ref, gpre_ref, gpost_ref, win_ref, cw_ref, wout_ref, o_ref,
                 h_ref, u_ref, z_ref, *, tc, rows):
    tm, d = x_ref.shape
    i = pl.program_id(0)
    shift, scale, gate = _mod_parts(mod_ref[...], 1, d)
    g_pre = gpre_ref[...]

    def pre(x):
        return (_rms(x, g_pre) * (1.0 + scale) + shift).astype(BF16)

    h_ref[0:HALO, :] = pre(xp_ref[...])
    h_ref[HALO:HALO + tm, :] = pre(x_ref[...])
    h_ref[HALO + tm:, :] = pre(xn_ref[...])

    seq_len = jnp.where(i < rows.n_ctx_tiles, rows.ctx_len, rows.lat_len)
    pos = (i * tm + lax.broadcasted_iota(jnp.int32, (tm, 1), 0)) & (seq_len - 1)
    has_left = pos != 0
    has_right = pos != seq_len - 1

    for c in range(d // tc):
        cols = slice(c * tc, (c + 1) * tc)
        b_gate = _dot(h_ref[HALO:HALO + tm, :], win_ref[:, c * tc:(c + 1) * tc].astype(BF16))
        c_gate = _dot(h_ref[...], win_ref[:, d + c * tc:d + (c + 1) * tc].astype(BF16))
        x_in = _dot(h_ref[...], win_ref[:, 2 * d + c * tc:2 * d + (c + 1) * tc].astype(BF16))
        u_ref[...] = c_gate * x_in
        cw = cw_ref[:, cols]
        y = (cw[0:1, :] * jnp.where(has_left, u_ref[HALO - 1:HALO - 1 + tm, :], 0.0)
             + cw[1:2, :] * u_ref[HALO:HALO + tm, :]
             + cw[2:3, :] * jnp.where(has_right, u_ref[HALO + 1:HALO + 1 + tm, :], 0.0))
        z_ref[:, cols] = (b_gate * y).astype(BF16)
    out = _dot(z_ref[...], wout_ref[...])
    o_ref[...] = x_ref[...] + gate * _rms(out, gpost_ref[...])


def _conv_mixer(x, mod_l, g_pre, g_post, w_in, conv_w, w_out, rows):
    m, d = x.shape
    tm = rows.tile
    tc = 256
    assert rows.ctx_len & (rows.ctx_len - 1) == 0 and rows.lat_len & (rows.lat_len - 1) == 0
    hb = tm // HALO
    last = m // HALO - 1
    return pl.pallas_call(
        functools.partial(_conv_kernel, tc=tc, rows=rows),
        out_shape=jax.ShapeDtypeStruct((m, d), F32),
        grid=(rows.n_tiles,),
        in_specs=[pl.BlockSpec((tm, d), lambda i: (i, 0)),
                  pl.BlockSpec((HALO, d), lambda i: (jnp.maximum(i * hb - 1, 0), 0)),
                  pl.BlockSpec((HALO, d), lambda i: (jnp.minimum((i + 1) * hb, last), 0)),
                  rows.mod_spec(mod_l.shape[-1]),
                  pl.BlockSpec((1, d), lambda i: (0, 0)),
                  pl.BlockSpec((1, d), lambda i: (0, 0)),
                  _resident_spec(*w_in),
                  pl.BlockSpec((CONV_W, d), lambda i: (0, 0)),
                  _resident_spec(*w_out)],
        out_specs=pl.BlockSpec((tm, d), lambda i: (i, 0)),
        scratch_shapes=[pltpu.VMEM((tm + 2 * HALO, d), BF16),
                        pltpu.VMEM((tm + 2 * HALO, tc), F32),
                        pltpu.VMEM((tm, d), BF16)],
        compiler_params=_params("parallel"),
        name="conv_mixer",
    )(x, x, x, mod_l, g_pre, g_post, w_in[0], conv_w, w_out[0])


def _scan_constants(c):
    t, s = np.meshgrid(np.arange(c), np.arange(c), indexing="ij")
    x = t ^ s
    n_levels = int(math.log2(c // SCAN_GROUP))
    code = np.full((c, c), n_levels, np.int32)
    for lvl in range(n_levels - 1, 0, -1):
        code = np.where(x >= (c >> lvl), lvl, code)
    code = np.where(x < SCAN_GROUP, n_levels + 1, code)
    visited = np.stack([s <= t, s >= t])
    codes = np.where(visited, code[None], 0).astype(np.int32)
    return jnp.asarray(codes), jnp.asarray(visited, BF16), n_levels


def _gate_chunk(z, lb, cum, k_ref, c_ref, base):
    c_len, dk = z.shape
    e_abs = jnp.exp(-jnp.abs(z))
    inv = 1.0 / (1.0 + e_abs)
    sig = jnp.where(z >= 0, inv, e_abs * inv)
    sig_neg = jnp.where(z >= 0, e_abs * inv, inv)
    f = lb + (1.0 - lb) * sig
    log_sig = jnp.minimum(z, 0.0) + jnp.log(inv)
    log_f = jnp.maximum(jnp.log1p(-lb) + log_sig, jnp.log(jnp.maximum(f, 1e-30))) * LOG2_E
    k_ref[pl.ds(base, c_len), :] = (1.0 - lb) * sig_neg

    hi = log_f.astype(BF16)
    rest = log_f - hi.astype(F32)
    mid = rest.astype(BF16)
    lo = (rest - mid.astype(F32)).astype(BF16)
    sums = _dot(cum, jnp.concatenate([hi, mid, lo], axis=1))
    c = sums[:, :dk] + sums[:, dk:2 * dk] + sums[:, 2 * dk:]
    c_ref[pl.ds(base, c_len), :] = c

    span = jnp.zeros((1, dk), F32)
    for g in range(0, c_len, SCAN_GROUP):
        ends = c_ref[pl.ds(base + g, 1), :] - c_ref[pl.ds(base + g + SCAN_GROUP - 1, 1), :]
        span = jnp.maximum(span, jnp.abs(ends))
    return span


def _scan_chunk(q, k, c, v, c_row, st_ref, kt_ref, code, n_levels, forward, factorised):
    c_len, dk = q.shape
    q16 = q.astype(BF16)
    k16 = k.astype(BF16)

    def reference_rows(block, offset):
        parts = [jnp.broadcast_to(c_row(n * block + offset), (block, dk)) for n in range(c_len // block)]
        return parts[0] if len(parts) == 1 else jnp.concatenate(parts, axis=0)

    def scores(slot, q_scaled, k_scaled):
        kt_ref[slot] = k_scaled.T
        return _dot(q_scaled, kt_ref[slot])

    att = jnp.zeros((c_len, c_len), F32)
    for lvl in range(1, n_levels + 1):
        half = c_len >> lvl
        e = jnp.exp2(-jnp.abs(c - reference_rows(2 * half, half))).astype(BF16)
        att = jnp.where(code == lvl, scores(lvl, q16 * e, k16 * e), att)

    if factorised:
        dq = c - reference_rows(SCAN_GROUP, 0 if forward else SCAN_GROUP - 1)
        same = scores(0, q16 * jnp.exp2(dq).astype(BF16), k16 * jnp.exp2(-dq).astype(BF16))
    else:
        row = lax.broadcasted_iota(jnp.int32, (c_len, dk), 0)
        t = lax.broadcasted_iota(jnp.int32, (c_len, c_len), 0)
        s = lax.broadcasted_iota(jnp.int32, (c_len, c_len), 1)
        pos = row & (SCAN_GROUP - 1)
        same = jnp.zeros((c_len, c_len), F32)
        for dist in range(SCAN_GROUP):
            if dist == 0:
                k_s, c_s = k, c
            else:
                shift = dist if forward else c_len - dist
                k_s, c_s = pltpu.roll(k, shift, 0), pltpu.roll(c, shift, 0)
            valid = (pos >= dist) if forward else (pos + dist < SCAN_GROUP)
            term = q * k_s * jnp.exp2(jnp.where(valid, c - c_s, 0.0))
            a = jnp.sum(jnp.where(valid, term, 0.0), axis=-1, keepdims=True)
            same = jnp.where(s == (t - dist if forward else t + dist), a, same)
    att = jnp.where(code == n_levels + 1, same, att)

    st = st_ref[...]
    v16 = v.astype(BF16)
    o = _dot(att.astype(BF16), v16) + _dot(q16 * jnp.exp2(c).astype(BF16), st.astype(BF16))
    c_end = c_row(c_len - 1 if forward else 0)
    row_decay = jnp.broadcast_to(jnp.exp2(c_end), (dk, dk)).T
    st_ref[...] = st * row_decay + _dot_tn(k16 * jnp.exp2(c_end - c).astype(BF16), v16)
    return o


def _hgrn_kernel(*refs, layer_j, chunk, n_levels, heads, has_s0, n_prior, emit_state):
    q_ref, zf_ref, zb_ref, v_ref, g_ref, alb_ref, gn_ref, code_ref, sums_ref = refs[:9]
    n_in = 9 + int(has_s0) + (n_prior or 0)
    o_ref = refs[n_in]
    s_ref, o_dir_ref, k_ref, c_ref, kt_ref = refs[-5:]
    seq = q_ref.shape[0]
    dk = q_ref.shape[1] // heads
    n_chunks = seq // chunk
    z_refs = (zf_ref, zb_ref)

    def chunk_base(n):
        return pl.multiple_of(n * chunk, chunk)

    def lanes(head):
        return slice(head * dk, (head + 1) * dk)

    a = alb_ref[...]
    e = jnp.exp(a - jnp.max(a, axis=0, keepdims=True))
    p = e / jnp.sum(e, axis=0, keepdims=True)
    cum = p[0]
    for n in range(1, layer_j + 1):
        cum = cum + p[n]
    lb = jnp.clip(cum - p[0], 0.0, LB_MAX)

    def gates(n, span):
        base = chunk_base(n)
        rows = pl.ds(base, chunk)
        for head in range(heads):
            for direction in range(2):
                span = jnp.maximum(span, _gate_chunk(
                    z_refs[direction][rows, lanes(head)], lb[direction:direction + 1, lanes(head)],
                    sums_ref[direction], k_ref.at[direction, head], c_ref.at[direction, head], base))
        return span

    unroll = 2 if n_chunks % 2 == 0 else 1
    span = lax.fori_loop(0, n_chunks, gates, jnp.zeros((1, dk), F32), unroll=unroll)
    safe = jnp.max(span) <= SAFE_EXP2

    for head in range(heads):
        for direction in range(2):
            s_ref[direction, head] = refs[9][direction, head] if has_s0 else jnp.zeros((dk, dk), F32)

    def scan(factorised):
        def body(n, carry):
            for head in range(heads):
                for direction in range(2):
                    base = chunk_base(n if direction == 0 else n_chunks - 1 - n)
                    rows = pl.ds(base, chunk)
                    c_dir = c_ref.at[direction, head]
                    o_dir_ref[direction, head, rows, :] = _scan_chunk(
                        q_ref[rows, lanes(head)], k_ref[direction, head, rows, :], c_dir[rows, :],
                        v_ref[rows, lanes(head)], lambda r, c_dir=c_dir, base=base: c_dir[pl.ds(base + r, 1), :],
                        s_ref.at[direction, head], kt_ref.at[direction, head], code_ref[direction], n_levels,
                        direction == 0, factorised)
            return carry

        lax.fori_loop(0, n_chunks, body, 0, unroll=unroll if factorised else 1)

    lax.cond(safe, lambda: scan(True), lambda: scan(False))

    for head in range(heads):
        o = o_dir_ref[0, head] + o_dir_ref[1, head]
        o_ref[:, lanes(head)] = _rms(o, gn_ref[...]) * _silu(g_ref[:, lanes(head)])
        if emit_state:
            so_ref = refs[n_in + 1]
            if n_prior is not None:
                for n in range(n_prior):
                    so_ref[n, :, head] = refs[9 + int(has_s0) + n][:, head]
                so_ref = so_ref.at[n_prior]
            for direction in range(2):
                so_ref[direction, head] = s_ref[direction, head]


def _state_spec(layer_j, heads):
    return pl.BlockSpec((None, None, 2, heads, LANES, LANES), lambda b, h: (b, layer_j, 0, h, 0, 0))


def _hgrn(proj, row_block0, seq, n_batch, a_lb, gnorm, layer_j, s0=None, emit_state=False, prior=None):
    dk = LANES
    heads = A_HEADS if seq <= 2 * SCAN_CHUNK else SCAN_HEADS_PER_STEP
    groups = A_HEADS // heads
    chunk = min(SCAN_CHUNK, seq)
    codes, sums, n_levels = _scan_constants(chunk)
    n_layers = a_lb.shape[0]

    def col(section):
        return pl.BlockSpec((seq, heads * dk), lambda b, h: (row_block0 + b, section * groups + h))

    def const(a):
        return pl.BlockSpec(a.shape, lambda b, h: (0,) * a.ndim)

    layer_state_spec = pl.BlockSpec((None, 2, heads, dk, dk), lambda b, h: (b, 0, h, 0, 0))
    extra, extra_specs = [], []
    if s0 is not None:
        extra.append(s0)
        extra_specs.append(_state_spec(layer_j, heads))
    if prior is not None:
        extra += list(prior)
        extra_specs += [layer_state_spec] * len(prior)
    out_shape = [jax.ShapeDtypeStruct((n_batch * seq, A_HEADS * dk), F32)]
    out_specs = [pl.BlockSpec((seq, heads * dk), lambda b, h: (b, h))]
    if emit_state and prior is not None:
        stacked = len(prior) + 1
        out_shape.append(jax.ShapeDtypeStruct((n_batch, stacked, 2, A_HEADS, dk, dk), F32))
        out_specs.append(pl.BlockSpec((None, stacked, 2, heads, dk, dk), lambda b, h: (b, 0, 0, h, 0, 0)))
    elif emit_state:
        out_shape.append(jax.ShapeDtypeStruct((n_batch, 2, A_HEADS, dk, dk), F32))
        out_specs.append(layer_state_spec)
    per_head = (2, heads, seq, dk)
    return pl.pallas_call(
        functools.partial(_hgrn_kernel, layer_j=layer_j, chunk=chunk, n_levels=n_levels, heads=heads,
                          has_s0=s0 is not None, n_prior=None if prior is None else len(prior),
                          emit_state=emit_state),
        out_shape=tuple(out_shape),
        grid=(n_batch, groups),
        in_specs=[col(0), col(1), col(2), col(3), col(4),
                  pl.BlockSpec((n_layers, 2, heads * dk), lambda b, h: (0, 0, h)),
                  pl.BlockSpec((1, dk), lambda b, h: (0, 0)), const(codes), const(sums)] + extra_specs,
        out_specs=tuple(out_specs),
        scratch_shapes=[pltpu.VMEM((2, heads, dk, dk), F32), pltpu.VMEM(per_head, F32),
                        pltpu.VMEM(per_head, F32), pltpu.VMEM(per_head, F32),
                        pltpu.VMEM((2, heads, n_levels + 1, dk, chunk), BF16)],
        compiler_params=_params("parallel", "parallel"),
        name="hgrn2_scan",
    )(proj, proj, proj, proj, proj, a_lb, gnorm, codes, sums, *extra)


def _head_lanes(shape, head):
    lane = lax.broadcasted_iota(jnp.int32, shape, len(shape) - 1)
    half = shape[-1] // 2
    return (lane < half) if head == 0 else (lane >= half)


def _ctx_attn_kernel(*refs, scale, n_prior):
    q_ref, k_ref, v_ref = refs[:3]
    o_ref, ko_ref, vo_ref = refs[-3:]
    dh = LANES // 2
    if n_prior is not None:
        for n in range(n_prior):
            ko_ref[n] = refs[3 + n][...]
            vo_ref[n] = refs[3 + n_prior + n][...]
        ko_ref, vo_ref = ko_ref.at[n_prior], vo_ref.at[n_prior]
    for pair in range(q_ref.shape[1] // LANES):
        lanes = slice(pair * LANES, (pair + 1) * LANES)
        k32 = k_ref[:, lanes]
        v32 = v_ref[:, lanes]
        k = k32.astype(BF16)
        v = v32.astype(BF16)
        q = q_ref[:, lanes]
        outs = []
        for head in range(2):
            ko_ref[2 * pair + head] = k32[:, head * dh:(head + 1) * dh]
            vo_ref[2 * pair + head] = v32[:, head * dh:(head + 1) * dh]
            qh = jnp.where(_head_lanes(q.shape, head), q, 0.0).astype(BF16)
            s = _dot_nt(qh, k) * scale
            p = jnp.exp(s - jnp.max(s, axis=-1, keepdims=True))
            outs.append(_dot(p.astype(BF16), v) / jnp.sum(p, axis=-1, keepdims=True))
        o_ref[:, lanes] = jnp.where(_head_lanes(q.shape, 0), outs[0], outs[1])


def _ctx_attention(proj, n_batch, seq, col0, prior, last):
    width = B_HEADS // 2 * LANES
    dh = LANES // 2
    cb = col0 // width
    assert col0 % width == 0
    layer_spec = pl.BlockSpec((None, B_HEADS, seq, dh), lambda b: (b, 0, 0, 0))
    if last:
        n_layers = len(prior) + 1
        cache = jax.ShapeDtypeStruct((n_batch, n_layers, B_HEADS, seq, dh), F32)
        cache_spec = pl.BlockSpec((None, n_layers, B_HEADS, seq, dh), lambda b: (b, 0, 0, 0, 0))
        earlier = [kv[0] for kv in prior] + [kv[1] for kv in prior]
    else:
        cache = jax.ShapeDtypeStruct((n_batch, B_HEADS, seq, dh), F32)
        cache_spec = layer_spec
        earlier = []

    def col(section):
        return pl.BlockSpec((seq, width), lambda b: (b, cb + section))

    return pl.pallas_call(
        functools.partial(_ctx_attn_kernel, scale=1.0 / math.sqrt(dh), n_prior=len(prior) if last else None),
        out_shape=(jax.ShapeDtypeStruct((n_batch * seq, width), F32), cache, cache),
        grid=(n_batch,),
        in_specs=[col(0), col(1), col(2)] + [layer_spec] * len(earlier),
        out_specs=(pl.BlockSpec((seq, width), lambda b: (b, 0)), cache_spec, cache_spec),
        compiler_params=_params("parallel"),
        name="context_attention",
    )(proj, proj, proj, *earlier)


def _rpb_tiles_kernel(rpb_ref, o_ref):
    h = pl.program_id(0)
    qc = lax.broadcasted_iota(jnp.int32, (GRID_W, GRID_W), 0)
    kc = lax.broadcasted_iota(jnp.int32, (GRID_W, GRID_W), 1)
    c0 = jnp.clip(qc - NA_COLS // 2, 0, GRID_W - NA_COLS)
    col_ok = (kc >= c0) & (kc < c0 + NA_COLS)
    dc = jnp.clip(kc - qc, -(NA_COLS - 1), NA_COLS - 1) + NA_COLS - 1
    for dr in range(2 * NA_ROWS - 1):
        t = jnp.zeros((GRID_W, GRID_W), F32)
        for i in range(2 * NA_COLS - 1):
            t = jnp.where(dc == i, rpb_ref[h, dr, i], t)
        o_ref[dr] = jnp.where(col_ok, t, MASKED)


def _rpb_tiles(rpb):
    heads, n_dr, n_dc = rpb.shape
    return pl.pallas_call(
        _rpb_tiles_kernel,
        out_shape=jax.ShapeDtypeStruct((heads, n_dr, GRID_W, GRID_W), F32),
        grid=(heads,),
        in_specs=[pl.BlockSpec(memory_space=pltpu.SMEM)],
        out_specs=pl.BlockSpec((None, n_dr, GRID_W, GRID_W), lambda h: (h, 0, 0, 0)),
        compiler_params=_params("parallel"),
        name="rpb_tiles",
    )(rpb)


def _nbr_geometry(grid_rows):
    win = min(NA_ROWS, grid_rows)
    blocks = []
    for qb in range(grid_rows // Q_ROWS):
        r0 = qb * Q_ROWS
        key0 = min(max(r0 - win // 2, 0), grid_rows - KEY_ROWS)
        tiles = []
        for i in range(Q_ROWS):
            r = r0 + i
            start = min(max(r - win // 2, 0), grid_rows - win)
            assert key0 <= start and start + win <= key0 + KEY_ROWS
            tiles.append([(key0 + jr - r + NA_ROWS - 1) if start <= key0 + jr < start + win else None
                          for jr in range(KEY_ROWS)])
        blocks.append((key0, tiles))
    return blocks


def _nbr_attn_kernel(q_ref, k_ref, v_ref, ck_ref, cv_ref, tiles_ref, o_ref, bias_ref, *, scale, geometry):
    w = GRID_W
    nq = Q_ROWS * w
    nk = KEY_ROWS * w

    @pl.when(pl.program_id(1) == 0)
    def _():
        masked = jnp.full((w, w), MASKED, F32)
        for head in range(2):
            for qb, (_, tiles) in enumerate(geometry):
                for i, row_tiles in enumerate(tiles):
                    for jr, dr in enumerate(row_tiles):
                        tile = masked if dr is None else tiles_ref[head, dr]
                        bias_ref[head, qb, i * w:(i + 1) * w, jr * w:(jr + 1) * w] = tile

    ck = ck_ref[...].astype(BF16)
    cv = cv_ref[...].astype(BF16)
    for qb, (key0, _) in enumerate(geometry):
        q = q_ref[qb * nq:(qb + 1) * nq, :] * scale
        k = k_ref[key0 * w:key0 * w + nk, :].astype(BF16)
        v = v_ref[key0 * w:key0 * w + nk, :].astype(BF16)
        outs = []
        for head in range(2):
            qh = jnp.where(_head_lanes(q.shape, head), q, 0.0).astype(BF16)
            s_loc = _dot_nt(qh, k) + bias_ref[head, qb]
            s_ctx = _dot_nt(qh, ck)
            m = jnp.maximum(jnp.max(s_loc, axis=-1, keepdims=True), jnp.max(s_ctx, axis=-1, keepdims=True))
            p_loc = jnp.exp(s_loc - m)
            p_ctx = jnp.exp(s_ctx - m)
            denom = jnp.sum(p_loc, axis=-1, keepdims=True) + jnp.sum(p_ctx, axis=-1, keepdims=True)
            outs.append((_dot(p_loc.astype(BF16), v) + _dot(p_ctx.astype(BF16), cv)) / denom)
        o_ref[qb * nq:(qb + 1) * nq, :] = jnp.where(_head_lanes(q.shape, 0), outs[0], outs[1])


def _nbr_attention(proj, row_block0, seq, n_batch, col0, ck, cv, tiles):
    pairs = B_HEADS // 2
    cb = col0 // LANES
    past = ck.shape[2]
    grid_rows = seq // GRID_W
    assert grid_rows % Q_ROWS == 0 and grid_rows >= KEY_ROWS
    geometry = _nbr_geometry(grid_rows)
    nq, nk = Q_ROWS * GRID_W, KEY_ROWS * GRID_W

    def col(section):
        return pl.BlockSpec((seq, LANES), lambda h, b: (row_block0 + b, cb + section * pairs + h))

    return pl.pallas_call(
        functools.partial(_nbr_attn_kernel, scale=1.0 / math.sqrt(LANES // 2), geometry=geometry),
        out_shape=jax.ShapeDtypeStruct((n_batch * seq, pairs * LANES), F32),
        grid=(pairs, n_batch),
        in_specs=[col(0), col(1), col(2),
                  pl.BlockSpec((None, None, past, LANES), lambda h, b: (b, h, 0, 0)),
                  pl.BlockSpec((None, None, past, LANES), lambda h, b: (b, h, 0, 0)),
                  pl.BlockSpec((2,) + tiles.shape[1:], lambda h, b: (h, 0, 0, 0))],
        out_specs=pl.BlockSpec((seq, LANES), lambda h, b: (b, h)),
        scratch_shapes=[pltpu.VMEM((2, len(geometry), nq, nk), F32)],
        compiler_params=_params("arbitrary", "arbitrary"),
        name="neighbourhood_attention",
    )(proj, proj, proj, ck, cv, tiles)


def _head_pairs(t):
    b, h, l, dh = t.shape
    return t.reshape(b, h // 2, 2, l, dh).transpose(0, 1, 3, 2, 4).reshape(b, h // 2, l, 2 * dh)


def kernel(x_prompt, x_sample, c, state_hgrn, cache_k, cache_v, c_ctx, mod_w, mod_b, norm_pre, norm_post,
           ffn_w_in, ffn_w_out, ab_w_in, ab_w_out, a_lb, a_gnorm, b_rpb, cv_w_in, cv_conv, cv_w_out):
    n_ctx, ctx_len, d = x_prompt.shape
    n_lat, lat_len, _ = x_sample.shape
    depth = mod_w.shape[0]
    d_a = a_lb.shape[-1]
    rows = _Rows(n_ctx * ctx_len, ctx_len, n_lat * lat_len, lat_len, ROW_TILE)
    ffn_rows = _Rows(n_ctx * ctx_len, ctx_len, n_lat * lat_len, lat_len, FFN_ROW_TILE)
    assert (n_ctx * ctx_len) % lat_len == 0
    lat_block0 = (n_ctx * ctx_len) // lat_len

    x = (x_prompt.reshape(-1, d), x_sample.reshape(-1, d))

    cond = jnp.concatenate([c_ctx[None, :], c], axis=0)
    cond = jnp.pad(cond, ((0, -cond.shape[0] % 8), (0, 0)))
    mod = _modulation(cond, mod_w, mod_b)
    mod = mod.reshape(depth, cond.shape[0], 1, N_MOD * d)

    ffn_w_out = ffn_w_out.astype(BF16)
    ab_w_out = ab_w_out.astype(BF16)
    cv_w_out = cv_w_out.astype(BF16)

    n_even = a_lb.shape[0]
    layer_states, layer_caches = [], []
    for l in range(depth):
        j = l // 2

        def gains(t, sub):
            return t[l, sub][None, :]

        x = _ffn(x, mod[l], gains(norm_pre, 0), gains(norm_post, 0), (ffn_w_in, (l, 0)), (ffn_w_out, (l, 0)),
                 ffn_rows, 0)
        if l % 2 == 0:
            proj = _inproj(x, mod[l], gains(norm_pre, 1), (ab_w_in, (j,)), rows)
            gn = a_gnorm[j][None, :]
            last = j == n_even - 1
            oa_ctx, new_state = _hgrn(proj, 0, ctx_len, n_ctx, a_lb, gn, j, emit_state=True,
                                      prior=layer_states if last else None)
            oa_lat, = _hgrn(proj, lat_block0, lat_len, n_lat, a_lb, gn, j, s0=state_hgrn)
            ob_ctx, *new_caches = _ctx_attention(proj, n_ctx, ctx_len, 5 * d_a, layer_caches, last)
            layer_states.append(new_state)
            layer_caches.append(new_caches)
            ob_lat = _nbr_attention(proj, lat_block0, lat_len, n_lat, 5 * d_a,
                                    _head_pairs(cache_k[:, j]), _head_pairs(cache_v[:, j]),
                                    _rpb_tiles(b_rpb[j]))
            mixer = ((oa_ctx, oa_lat), (ob_ctx, ob_lat), (ab_w_out, (j,)), gains(norm_post, 1))
        else:
            x = _conv_mixer(x, mod[l], gains(norm_pre, 1), gains(norm_post, 1), (cv_w_in, (j,)), cv_conv[j],
                            (cv_w_out, (j,)), rows)
            mixer = None
        x = _ffn(x, mod[l], gains(norm_pre, 2), gains(norm_post, 2), (ffn_w_in, (l, 1)), (ffn_w_out, (l, 1)),
                 ffn_rows, 2, split_out=(l == depth - 1), mixer=mixer)

    return (x[0].reshape(x_prompt.shape), x[1].reshape(x_sample.shape), new_state, *new_caches)
```
